```python
import jax, jax.numpy as jnp
from jax import lax
import numpy as np

D_MODEL = 2048
BATCH = 2
SEQ = 8192
DEPTH = 4

N_A_LAYERS = DEPTH // 2
N_B_LAYERS = DEPTH - N_A_LAYERS
D_PLE = 256
D_FF = 5504
MLSTM_HEADS = 4
MLSTM_DQK = D_MODEL // (2 * MLSTM_HEADS)
MLSTM_DV = D_MODEL // MLSTM_HEADS
MLSTM_CHUNK = 64
MLSTM_PROJ = 2 * MLSTM_HEADS * MLSTM_DQK + 2 * MLSTM_HEADS * MLSTM_DV + 2 * MLSTM_HEADS
FORGET_BIAS_OFFSET = 3.0
SB_HEADS = 16
SB_HEAD_DIM = D_MODEL // SB_HEADS
SB_BLOCK = 128
RMS_EPS = 1e-6

kernel_name = "yoco_mlstm_stickbreaking_macaron"


def rmsnorm(x, g):
    xf = x.astype(jnp.float32)
    y = xf * lax.rsqrt(jnp.mean(xf * xf, axis=-1, keepdims=True) + RMS_EPS)
    return (y * g.astype(jnp.float32)).astype(x.dtype)


def swiglu(x, w_gate, w_up, w_down):
    return (jax.nn.silu(x @ w_gate) * (x @ w_up)) @ w_down


def mlstm_mixer(h, w_in, b_if, g_head, w_out):
    B, S, _ = h.shape
    H, dk, dv, L = MLSTM_HEADS, MLSTM_DQK, MLSTM_DV, MLSTM_CHUNK
    NC = S // L
    proj = (h @ w_in).astype(jnp.float32)
    o1 = H * dk; o2 = 2 * H * dk; o3 = o2 + H * dv; o4 = o3 + H * dv; o5 = o4 + H
    q, k, v, og, gi, gf = jnp.split(proj, [o1, o2, o3, o4, o5], axis=-1)
    b_if = b_if.astype(jnp.float32)
    log_i = gi + b_if[:H]
    log_f = jax.nn.log_sigmoid(gf + b_if[H:])
    q = q * (dk ** -0.5)

    def to_chunks(t, d):
        return t.reshape(B, NC, L, H, d).transpose(1, 0, 3, 2, 4)

    def gate_chunks(t):
        return t.reshape(B, NC, L, H).transpose(1, 0, 3, 2)

    xs = (to_chunks(q, dk), to_chunks(k, dk), to_chunks(v, dv), gate_chunks(log_i), gate_chunks(log_f))
    causal = jnp.tril(jnp.ones((L, L), dtype=bool))

    def chunk_step(carry, inp):
        C, n, m = carry
        qc, kc, vc, ic, fc = inp
        b = jnp.cumsum(fc, axis=-1)
        a = b + m[..., None]
        Dm = jnp.where(causal, b[..., :, None] - b[..., None, :] + ic[..., None, :], -jnp.inf)
        m_t = jnp.maximum(a, jnp.max(Dm, axis=-1))
        w_inter = jnp.exp(a - m_t)
        s_qk = jnp.einsum('bhtd,bhsd->bhts', qc, kc) * jnp.exp(Dm - m_t[..., None])
        num = w_inter[..., None] * jnp.einsum('bhtd,bhde->bhte', qc, C) + jnp.einsum('bhts,bhse->bhte', s_qk, vc)
        den = w_inter * jnp.einsum('bhtd,bhd->bht', qc, n) + jnp.sum(s_qk, axis=-1)
        h_out = num / jnp.maximum(jnp.abs(den), jnp.exp(-m_t))[..., None]
        bL = b[..., -1]
        g_end = bL[..., None] - b + ic
        m_new = jnp.maximum(bL + m, jnp.max(g_end, axis=-1))
        decay = jnp.exp(bL + m - m_new)
        w_end = jnp.exp(g_end - m_new[..., None])
        C_new = decay[..., None, None] * C + jnp.einsum('bhs,bhsd,bhse->bhde', w_end, kc, vc)
        n_new = decay[..., None] * n + jnp.einsum('bhs,bhsd->bhd', w_end, kc)
        return (C_new, n_new, m_new), h_out

    init = (jnp.zeros((B, H, dk, dv), jnp.float32), jnp.zeros((B, H, dk), jnp.float32), jnp.zeros((B, H), jnp.float32))
    _, hs = lax.scan(chunk_step, init, xs)
    hs = hs.transpose(1, 0, 3, 2, 4).reshape(B, S, H, dv)
    hs = rmsnorm(hs, g_head.reshape(H, dv)).reshape(B, S, H * dv)
    hs = hs * jax.nn.sigmoid(og)
    return hs.astype(h.dtype) @ w_out


def shared_kv(h, g_kv, w_kv, g_k):
    B, S, _ = h.shape
    kv = (rmsnorm(h, g_kv) @ w_kv).astype(jnp.float32)
    k, v = jnp.split(kv, 2, axis=-1)
    k = rmsnorm(k.reshape(B, S, SB_HEADS, SB_HEAD_DIM), g_k)
    v = v.reshape(B, S, SB_HEADS, SB_HEAD_DIM)
    return k.transpose(0, 2, 1, 3), v.transpose(0, 2, 1, 3)


def stick_breaking_mixer(h, w_q, g_q, w_o, k, v):
    B, S, _ = h.shape
    q = (h @ w_q).astype(jnp.float32).reshape(B, S, SB_HEADS, SB_HEAD_DIM)
    q = (rmsnorm(q, g_q) * (SB_HEAD_DIM ** -0.5)).transpose(0, 2, 1, 3)
    outs = []
    for blk in range(S // SB_BLOCK):
        t0 = blk * SB_BLOCK
        kn = t0 + SB_BLOCK
        z = jnp.einsum('bhtd,bhsd->bhts', q[:, :, t0:kn], k[:, :, :kn])
        mask = jnp.arange(kn)[None, :] < (t0 + jnp.arange(SB_BLOCK))[:, None]
        log_beta = jax.nn.log_sigmoid(z)
        log_keep = jnp.where(mask, jax.nn.log_sigmoid(-z), 0.0)
        after = lax.cumsum(log_keep, axis=log_keep.ndim - 1, reverse=True) - log_keep
        A = jnp.where(mask, jnp.exp(log_beta + after), 0.0)
        outs.append(jnp.einsum('bhts,bhsd->bhtd', A, v[:, :, :kn]))
    o = jnp.concatenate(outs, axis=2).transpose(0, 2, 1, 3).reshape(B, S, SB_HEADS * SB_HEAD_DIM)
    return o.astype(h.dtype) @ w_o


def per_layer_embedding(h, p_i, g, w_proj, w_gate):
    gate = jax.nn.sigmoid(rmsnorm(h, g) @ w_gate)
    return h + (p_i.astype(h.dtype) @ w_proj) * gate


def setup_inputs(seed: int = 0) -> dict:
    key = jax.random.key(seed)
    ks = jax.random.split(key, 24)
    f32 = jnp.float32

    def nrm(k, shape, scale):
        return jax.random.normal(k, shape, f32) * scale

    def gain(k, shape):
        return 1.0 + 0.02 * jax.random.normal(k, shape, f32)

    H = MLSTM_HEADS
    b_i = 0.1 * jax.random.normal(ks[8], (N_A_LAYERS, H), f32)
    b_f = FORGET_BIAS_OFFSET + 0.1 * jax.random.normal(ks[9], (N_A_LAYERS, H), f32)
    return {
        "x": nrm(ks[0], (BATCH, SEQ, D_MODEL), 1.0),
        "p": nrm(ks[1], (DEPTH, BATCH, SEQ, D_PLE), 1.0),
        "ffn_norm": gain(ks[2], (DEPTH, 2, D_MODEL)),
        "ffn_w_gate": nrm(ks[3], (DEPTH, 2, D_MODEL, D_FF), D_MODEL ** -0.5),
        "ffn_w_up": nrm(ks[4], (DEPTH, 2, D_MODEL, D_FF), D_MODEL ** -0.5),
        "ffn_w_down": nrm(ks[5], (DEPTH, 2, D_FF, D_MODEL), D_FF ** -0.5),
        "mix_norm": gain(ks[6], (DEPTH, D_MODEL)),
        "mlstm_w_in": nrm(ks[7], (N_A_LAYERS, D_MODEL, MLSTM_PROJ), D_MODEL ** -0.5),
        "mlstm_b_if": jnp.concatenate([b_i, b_f], axis=-1),
        "mlstm_head_norm": gain(ks[10], (N_A_LAYERS, H * MLSTM_DV)),
        "mlstm_w_out": nrm(ks[11], (N_A_LAYERS, H * MLSTM_DV, D_MODEL), (H * MLSTM_DV) ** -0.5),
        "kv_norm": gain(ks[12], (D_MODEL,)),
        "sb_w_kv": nrm(ks[13], (D_MODEL, 2 * SB_HEADS * SB_HEAD_DIM), D_MODEL ** -0.5),
        "sb_k_norm": gain(ks[14], (SB_HEAD_DIM,)),
        "sb_w_q": nrm(ks[15], (N_B_LAYERS, D_MODEL, SB_HEADS * SB_HEAD_DIM), D_MODEL ** -0.5),
        "sb_q_norm": gain(ks[16], (N_B_LAYERS, SB_HEAD_DIM)),
        "sb_w_o": nrm(ks[17], (N_B_LAYERS, SB_HEADS * SB_HEAD_DIM, D_MODEL), (SB_HEADS * SB_HEAD_DIM) ** -0.5),
        "ple_norm": gain(ks[18], (DEPTH, D_MODEL)),
        "ple_w_proj": nrm(ks[19], (DEPTH, D_PLE, D_MODEL), D_PLE ** -0.5),
        "ple_w_gate": nrm(ks[20], (DEPTH, D_MODEL, D_MODEL), D_MODEL ** -0.5),
    }


def reference(x, p, ffn_norm, ffn_w_gate, ffn_w_up, ffn_w_down, mix_norm,
              mlstm_w_in, mlstm_b_if, mlstm_head_norm, mlstm_w_out,
              kv_norm, sb_w_kv, sb_k_norm, sb_w_q, sb_q_norm, sb_w_o,
              ple_norm, ple_w_proj, ple_w_gate):
    h = x
    k_sh = None
    v_sh = None
    for i in range(DEPTH):
        if i == N_A_LAYERS:
            k_sh, v_sh = shared_kv(h, kv_norm, sb_w_kv, sb_k_norm)
        h = h + 0.5 * swiglu(rmsnorm(h, ffn_norm[i, 0]), ffn_w_gate[i, 0], ffn_w_up[i, 0], ffn_w_down[i, 0])
        hn = rmsnorm(h, mix_norm[i])
        if i < N_A_LAYERS:
            h = h + mlstm_mixer(hn, mlstm_w_in[i], mlstm_b_if[i], mlstm_head_norm[i], mlstm_w_out[i])
        else:
            j = i - N_A_LAYERS
            h = h + stick_breaking_mixer(hn, sb_w_q[j], sb_q_norm[j], sb_w_o[j], k_sh, v_sh)
        h = h + 0.5 * swiglu(rmsnorm(h, ffn_norm[i, 1]), ffn_w_gate[i, 1], ffn_w_up[i, 1], ffn_w_down[i, 1])
        h = per_layer_embedding(h, p[i], ple_norm[i], ple_w_proj[i], ple_w_gate[i])
    return h
```

```python
import functools

import jax
import jax.numpy as jnp
from jax import lax
from jax.experimental import pallas as pl
from jax.experimental.pallas import tpu as pltpu

F32 = jnp.float32
BF16 = jnp.bfloat16

RMS_EPS = 1e-6
FORGET_HEADS = 4
SB_HEAD_DIM = 128
LANES = 128
MXU_DIM = 256
VMEM_LIMIT = 56 * 1024 * 1024

ROW_TILE = 1024
NORM_ROWS = 256
MLSTM_CHUNK = 256
SB_TILE = 256
SB_SUB = 2
F32_EXP_ZERO = -104.0


def _params(*sem):
    return pltpu.CompilerParams(dimension_semantics=sem, vmem_limit_bytes=VMEM_LIMIT)


def _sigmoid(x):
    return 1.0 / (1.0 + jnp.exp(-x))


def _softplus_neg_abs(x):
    return jnp.log1p(jnp.exp(-jnp.abs(x)))


def _rmsnorm_rows(x_ref, g_ref, xn_ref):
    rows = x_ref.shape[0]
    step = min(NORM_ROWS, rows)

    def body(r, carry):
        sl = pl.ds(pl.multiple_of(r * step, step), step)
        x = x_ref[sl, :]
        ms = jnp.mean(x * x, axis=-1, keepdims=True)
        xn_ref[sl, :] = (x * lax.rsqrt(ms + RMS_EPS) * g_ref[...]).astype(BF16)
        return carry

    lax.fori_loop(0, rows // step, body, 0)


def _ffn_kernel(x_ref, g_ref, wg_ref, wu_ref, wd_ref, o_ref, xn_ref, *, col_chunk):
    j = pl.program_id(1)

    @pl.when(j == 0)
    def _():
        _rmsnorm_rows(x_ref, g_ref, xn_ref)
        o_ref[...] = x_ref[...]

    xn = xn_ref[...]
    gate = jnp.dot(xn, wg_ref[...], preferred_element_type=F32)
    up = jnp.dot(xn, wu_ref[...], preferred_element_type=F32)
    hid = (gate * _sigmoid(gate) * up * 0.5).astype(BF16)
    d = o_ref.shape[1]
    for c in range(d // col_chunk):
        cs = slice(c * col_chunk, (c + 1) * col_chunk)
        o_ref[:, cs] += jnp.dot(hid, wd_ref[:, cs], preferred_element_type=F32)


def _ffn(h, g, wg, wu, wd, *, tm, tf):
    t, d = h.shape
    ffp = wg.shape[1]
    return pl.pallas_call(
        functools.partial(_ffn_kernel, col_chunk=512),
        grid=(t // tm, ffp // tf),
        in_specs=[
            pl.BlockSpec((tm, d), lambda i, j: (i, 0)),
            pl.BlockSpec((1, d), lambda i, j: (0, 0)),
            pl.BlockSpec((d, tf), lambda i, j: (0, j)),
            pl.BlockSpec((d, tf), lambda i, j: (0, j)),
            pl.BlockSpec((tf, d), lambda i, j: (j, 0)),
        ],
        out_specs=pl.BlockSpec((tm, d), lambda i, j: (i, 0)),
        out_shape=jax.ShapeDtypeStruct((t, d), F32),
        scratch_shapes=[pltpu.VMEM((tm, d), BF16)],
        compiler_params=_params("parallel", "arbitrary"),
        name="ffn",
    )(h, g.reshape(1, d), wg, wu, wd)


def _head_norm(y, gain_row, scale):
    outs = []
    for s in range(y.shape[1] // SB_HEAD_DIM):
        ys = y[:, s * SB_HEAD_DIM:(s + 1) * SB_HEAD_DIM]
        ms = jnp.mean(ys * ys, axis=-1, keepdims=True)
        outs.append(ys * lax.rsqrt(ms + RMS_EPS) * gain_row * scale)
    return jnp.concatenate(outs, axis=1) if len(outs) > 1 else outs[0]


def _norm_proj_kernel(x_ref, g_ref, w_ref, hg_ref, o_ref, xn_ref, *, normed_blocks, scale):
    j = pl.program_id(1)

    @pl.when(j == 0)
    def _():
        _rmsnorm_rows(x_ref, g_ref, xn_ref)

    y = jnp.dot(xn_ref[...], w_ref[...], preferred_element_type=F32)
    if normed_blocks == 0:
        o_ref[...] = y.astype(o_ref.dtype)
    else:
        @pl.when(j < normed_blocks)
        def _():
            o_ref[...] = _head_norm(y, hg_ref[...], scale).astype(o_ref.dtype)

        @pl.when(j >= normed_blocks)
        def _():
            o_ref[...] = y.astype(o_ref.dtype)


def _norm_proj(h, g, w, *, out_dtype, tm, tn, head_gain=None, normed_cols=0, scale=1.0, name):
    t, d = h.shape
    n = w.shape[1]
    if head_gain is None:
        head_gain = jnp.ones((SB_HEAD_DIM,), F32)
    return pl.pallas_call(
        functools.partial(_norm_proj_kernel, normed_blocks=normed_cols // tn, scale=scale),
        grid=(t // tm, n // tn),
        in_specs=[
            pl.BlockSpec((tm, d), lambda i, j: (i, 0)),
            pl.BlockSpec((1, d), lambda i, j: (0, 0)),
            pl.BlockSpec((d, tn), lambda i, j: (0, j)),
            pl.BlockSpec((1, SB_HEAD_DIM), lambda i, j: (0, 0)),
        ],
        out_specs=pl.BlockSpec((tm, tn), lambda i, j: (i, j)),
        out_shape=jax.ShapeDtypeStruct((t, n), out_dtype),
        scratch_shapes=[pltpu.VMEM((tm, d), BF16)],
        compiler_params=_params("parallel", "arbitrary"),
        name=name,
    )(h, g.reshape(1, d), w, head_gain.reshape(1, SB_HEAD_DIM).astype(F32))


def _proj_res_kernel(a_ref, w_ref, r_ref, o_ref):
    o_ref[...] = r_ref[...] + jnp.dot(a_ref[...], w_ref[...], preferred_element_type=F32)


def _proj_res(a, w, res, *, tm, tn, name):
    t, k = a.shape
    n = w.shape[1]
    return pl.pallas_call(
        _proj_res_kernel,
        grid=(t // tm, n // tn),
        in_specs=[
            pl.BlockSpec((tm, k), lambda i, j: (i, 0)),
            pl.BlockSpec((k, tn), lambda i, j: (0, j)),
            pl.BlockSpec((tm, tn), lambda i, j: (i, j)),
        ],
        out_specs=pl.BlockSpec((tm, tn), lambda i, j: (i, j)),
        out_shape=jax.ShapeDtypeStruct((t, n), F32),
        compiler_params=_params("parallel", "arbitrary"),
        name=name,
    )(a, w, res)


def _ple_kernel(x_ref, g_ref, wg_ref, p_ref, wp_ref, r_ref, o_ref, xn_ref):
    j = pl.program_id(1)

    @pl.when(j == 0)
    def _():
        _rmsnorm_rows(x_ref, g_ref, xn_ref)

    gate = _sigmoid(jnp.dot(xn_ref[...], wg_ref[...], preferred_element_type=F32))
    emb = jnp.dot(p_ref[...].astype(BF16), wp_ref[...], preferred_element_type=F32)
    o_ref[...] = r_ref[...] + emb * gate


def _ple(h, p_i, g, w_proj, w_gate, *, tm, tn):
    t, d = h.shape
    dp = p_i.shape[1]
    return pl.pallas_call(
        _ple_kernel,
        grid=(t // tm, d // tn),
        in_specs=[
            pl.BlockSpec((tm, d), lambda i, j: (i, 0)),
            pl.BlockSpec((1, d), lambda i, j: (0, 0)),
            pl.BlockSpec((d, tn), lambda i, j: (0, j)),
            pl.BlockSpec((tm, dp), lambda i, j: (i, 0)),
            pl.BlockSpec((dp, tn), lambda i, j: (0, j)),
            pl.BlockSpec((tm, tn), lambda i, j: (i, j)),
        ],
        out_specs=pl.BlockSpec((tm, tn), lambda i, j: (i, j)),
        out_shape=jax.ShapeDtypeStruct((t, d), F32),
        scratch_shapes=[pltpu.VMEM((tm, d), BF16)],
        compiler_params=_params("parallel", "arbitrary"),
        name="ple",
    )(h, g.reshape(1, d), w_gate, p_i, w_proj, h)


def _mlstm_kernel(q_ref, k_ref, v_ref, og_ref, gates_ref, bias_ref, hg_ref, o_ref,
                  c_ref, n_ref, m_ref, *, dk):
    head = pl.program_id(1)
    chunk = pl.program_id(2)
    L = q_ref.shape[0]

    @pl.when(chunk == 0)
    def _():
        c_ref[...] = jnp.zeros_like(c_ref)
        n_ref[...] = jnp.zeros_like(n_ref)
        m_ref[...] = jnp.zeros_like(m_ref)

    gates = gates_ref[...] + bias_ref[...]
    lane = lax.broadcasted_iota(jnp.int32, gates.shape, 1)
    gi = jnp.sum(jnp.where(lane == head, gates, 0.0), axis=1, keepdims=True)
    gf = jnp.sum(jnp.where(lane == head + FORGET_HEADS, gates, 0.0), axis=1, keepdims=True)
    lf = jnp.minimum(gf, 0.0) - _softplus_neg_abs(gf)

    row = lax.broadcasted_iota(jnp.int32, (L, L), 0)
    col = lax.broadcasted_iota(jnp.int32, (L, L), 1)
    eye = row == col
    causal = col <= row
    lf_b = jnp.broadcast_to(lf, (L, L))
    gi_b = jnp.broadcast_to(gi, (L, L))
    lf_row = jnp.sum(jnp.where(eye, lf_b, 0.0), axis=0, keepdims=True)
    gi_row = jnp.sum(jnp.where(eye, gi_b, 0.0), axis=0, keepdims=True)
    b_row = jnp.sum(jnp.where(row <= col, lf_b, 0.0), axis=0, keepdims=True)
    b_col = jnp.sum(jnp.where(causal, jnp.broadcast_to(lf_row, (L, L)), 0.0), axis=1, keepdims=True)

    m_prev = m_ref[...]
    a_col = b_col + m_prev
    dm = jnp.where(causal, b_col - b_row + gi_row, -jnp.inf)
    m_t = jnp.maximum(a_col, jnp.max(dm, axis=1, keepdims=True))
    w_inter = jnp.exp(a_col - m_t)
    decay_mat = jnp.exp(dm - m_t)

    q = q_ref[...] * (dk ** -0.5)
    k = k_ref[...]
    v = v_ref[...]
    s_qk = lax.dot_general(q, k, (((1,), (1,)), ((), ())), preferred_element_type=F32) * decay_mat
    c_prev = c_ref[...]
    num = w_inter * jnp.dot(q, c_prev.astype(BF16), preferred_element_type=F32) \
        + jnp.dot(s_qk.astype(BF16), v, preferred_element_type=F32)
    qn = jnp.sum(q.astype(F32) * n_ref[...], axis=1, keepdims=True)
    den = w_inter * qn + jnp.sum(s_qk, axis=1, keepdims=True)
    hs = num / jnp.maximum(jnp.abs(den), jnp.exp(-m_t))

    ms = jnp.mean(hs * hs, axis=-1, keepdims=True)
    hs = hs * lax.rsqrt(ms + RMS_EPS) * hg_ref[...]
    o_ref[...] = (hs * _sigmoid(og_ref[...].astype(F32))).astype(o_ref.dtype)

    b_last = b_col[L - 1:L, :]
    g_end = b_last - b_col + gi
    m_new = jnp.maximum(b_last + m_prev, jnp.max(g_end, axis=0, keepdims=True))
    decay = jnp.exp(b_last + m_prev - m_new)
    kw = k.astype(F32) * jnp.exp(g_end - m_new)
    c_ref[...] = decay * c_prev + lax.dot_general(
        kw.astype(BF16), v, (((0,), (0,)), ((), ())), preferred_element_type=F32)
    n_ref[...] = decay * n_ref[...] + jnp.sum(kw, axis=0, keepdims=True)
    m_ref[...] = m_new


def _mlstm_cell(proj, gates, bias_row, head_gain, *, batch, seq, heads, dk, dv, chunk):
    t = proj.shape[0]
    nc = seq // chunk
    kq = heads
    vq = 2 * heads * dk // dv
    oq = vq + heads
    row = lambda b, h, c: b * nc + c
    return pl.pallas_call(
        functools.partial(_mlstm_kernel, dk=dk),
        grid=(batch, heads, nc),
        in_specs=[
            pl.BlockSpec((chunk, dk), lambda b, h, c: (row(b, h, c), h)),
            pl.BlockSpec((chunk, dk), lambda b, h, c: (row(b, h, c), kq + h)),
            pl.BlockSpec((chunk, dv), lambda b, h, c: (row(b, h, c), vq + h)),
            pl.BlockSpec((chunk, dv), lambda b, h, c: (row(b, h, c), oq + h)),
            pl.BlockSpec((chunk, LANES), lambda b, h, c: (row(b, h, c), 0)),
            pl.BlockSpec((1, LANES), lambda b, h, c: (0, 0)),
            pl.BlockSpec((1, dv), lambda b, h, c: (0, h)),
        ],
        out_specs=pl.BlockSpec((chunk, dv), lambda b, h, c: (row(b, h, c), h)),
        out_shape=jax.ShapeDtypeStruct((t, heads * dv), BF16),
        scratch_shapes=[pltpu.VMEM((dk, dv), F32), pltpu.VMEM((1, dk), F32), pltpu.VMEM((1, 1), F32)],
        compiler_params=_params("parallel", "parallel", "arbitrary"),
        name="mlstm_cell",
    )(proj, proj, proj, proj, gates, bias_row, head_gain.reshape(1, heads * dv))


def _sb_kernel(q_ref, k_ref, v_ref, o_ref, *, tile, sub):
    i = pl.program_id(2)
    d = q_ref.shape[1]
    row = lax.broadcasted_iota(jnp.int32, (tile, tile), 0)
    col = lax.broadcasted_iota(jnp.int32, (tile, tile), 1)
    later = (row > col).astype(BF16)
    col_minus_row = col - row
    qs = [q_ref[r * tile:(r + 1) * tile, :] for r in range(sub)]

    def body(carry):
        n, _, cs, accs = carry
        new_cs, new_accs, pending = [], [], []
        for r in range(sub):
            qb = i * sub + r
            kb = qb - n
            kbc = jnp.maximum(kb, 0)
            start = pl.multiple_of(kbc * tile, tile)
            k = k_ref[pl.ds(start, tile), :]
            v = v_ref[pl.ds(start, tile), :]
            z = lax.dot_general(qs[r], k, (((1,), (1,)), ((), ())), preferred_element_type=F32)
            limit = jnp.where(kb >= 0, (qb - kbc) * tile, -2 * tile)
            mask = col_minus_row < limit
            sp = _softplus_neg_abs(z)
            log_beta = jnp.minimum(z, 0.0) - sp
            log_keep = jnp.where(mask, jnp.minimum(-z, 0.0) - sp, 0.0)
            hi = log_keep.astype(BF16)
            rem = log_keep - hi.astype(F32)
            mid = rem.astype(BF16)
            lo = (rem - mid.astype(F32)).astype(BF16)
            after = (jnp.dot(hi, later, preferred_element_type=F32)
                     + jnp.dot(mid, later, preferred_element_type=F32)
                     + jnp.dot(lo, later, preferred_element_type=F32))
            a = jnp.where(mask, jnp.exp(log_beta + after + cs[r]), 0.0)
            new_accs.append(accs[r] + jnp.dot(a.astype(BF16), v, preferred_element_type=F32))
            c_new = cs[r] + after[:, 0:1] + log_keep[:, 0:1]
            new_cs.append(c_new)
            c_max = jnp.max(c_new, axis=0, keepdims=True)
            pending.append(jnp.where(kb >= 1, c_max, -jnp.inf))
        worst = functools.reduce(jnp.maximum, pending)
        go = jnp.max(worst) >= F32_EXP_ZERO
        return n + 1, go, tuple(new_cs), tuple(new_accs)

    init = (jnp.int32(0), jnp.bool_(True),
            tuple(jnp.zeros((tile, 1), F32) for _ in range(sub)),
            tuple(jnp.zeros((tile, d), F32) for _ in range(sub)))
    _, _, _, accs = lax.while_loop(lambda c: c[1], body, init)
    for r in range(sub):
        o_ref[r * tile:(r + 1) * tile, :] = accs[r].astype(o_ref.dtype)


def _sb_attention(q, kv, *, batch, seq, heads, tile, sub):
    d = SB_HEAD_DIM
    tq = tile * sub
    return pl.pallas_call(
        functools.partial(_sb_kernel, tile=tile, sub=sub),
        grid=(batch, heads, seq // tq),
        in_specs=[
            pl.BlockSpec((None, tq, d), lambda b, h, i: (b, i, h)),
            pl.BlockSpec((None, seq, d), lambda b, h, i: (b, 0, h)),
            pl.BlockSpec((None, seq, d), lambda b, h, i: (b, 0, heads + h)),
        ],
        out_specs=pl.BlockSpec((None, tq, d), lambda b, h, i: (b, i, h)),
        out_shape=jax.ShapeDtypeStruct((batch, seq, heads * d), BF16),
        compiler_params=_params("parallel", "parallel", "arbitrary"),
        name="sb_attention",
    )(q, kv, kv)


def kernel(x, p, ffn_norm, ffn_w_gate, ffn_w_up, ffn_w_down, mix_norm, mlstm_w_in, mlstm_b_if,
           mlstm_head_norm, mlstm_w_out, kv_norm, sb_w_kv, sb_k_norm, sb_w_q, sb_q_norm, sb_w_o,
           ple_norm, ple_w_proj, ple_w_gate):
    batch, seq, d = x.shape
    depth = p.shape[0]
    n_a = mlstm_w_in.shape[0]
    t = batch * seq
    ff = ffn_w_gate.shape[-1]
    ffp = -(-ff // (2 * MXU_DIM)) * (2 * MXU_DIM)
    heads = FORGET_HEADS
    dv = mlstm_w_out.shape[1] // heads
    dk = (mlstm_w_in.shape[2] - 2 * heads * dv - 2 * heads) // (2 * heads)
    n_main = 2 * heads * dk + 2 * heads * dv
    sb_heads = sb_w_q.shape[2] // SB_HEAD_DIM
    tm = ROW_TILE

    pad_ff = ffp - ff
    wg_all = jnp.pad(ffn_w_gate.astype(BF16), ((0, 0), (0, 0), (0, 0), (0, pad_ff)))
    wu_all = jnp.pad(ffn_w_up.astype(BF16), ((0, 0), (0, 0), (0, 0), (0, pad_ff)))
    wd_all = jnp.pad(ffn_w_down.astype(BF16), ((0, 0), (0, 0), (0, pad_ff), (0, 0)))
    w_in_main = mlstm_w_in[:, :, :n_main].astype(BF16)
    w_in_gates = jnp.pad(mlstm_w_in[:, :, n_main:].astype(BF16), ((0, 0), (0, 0), (0, LANES - 2 * heads)))
    bias_rows = jnp.pad(mlstm_b_if.astype(F32), ((0, 0), (0, LANES - 2 * heads)))
    w_out = mlstm_w_out.astype(BF16)
    w_kv = sb_w_kv.astype(BF16)
    w_q = sb_w_q.astype(BF16)
    w_o = sb_w_o.astype(BF16)
    w_pp = ple_w_proj.astype(BF16)
    w_pg = ple_w_gate.astype(BF16)

    h = x.reshape(t, d)
    p2 = p.reshape(depth, t, p.shape[-1])
    kv_sh = None
    for i in range(depth):
        if i == n_a:
            kv_sh = _norm_proj(h, kv_norm, w_kv, out_dtype=BF16, tm=tm, tn=512, head_gain=sb_k_norm,
                               normed_cols=sb_heads * SB_HEAD_DIM, name="kv_proj").reshape(batch, seq, -1)
        h = _ffn(h, ffn_norm[i, 0], wg_all[i, 0], wu_all[i, 0], wd_all[i, 0], tm=tm, tf=2 * MXU_DIM)
        if i < n_a:
            proj = _norm_proj(h, mix_norm[i], w_in_main[i], out_dtype=BF16, tm=tm, tn=1024, name="mlstm_in")
            gates = _norm_proj(h, mix_norm[i], w_in_gates[i], out_dtype=F32, tm=tm, tn=LANES, name="mlstm_gates")
            hs = _mlstm_cell(proj, gates, bias_rows[i:i + 1], mlstm_head_norm[i], batch=batch, seq=seq,
                             heads=heads, dk=dk, dv=dv, chunk=MLSTM_CHUNK)
            h = _proj_res(hs, w_out[i], h, tm=tm, tn=1024, name="mlstm_out")
        else:
            j = i - n_a
            q = _norm_proj(h, mix_norm[i], w_q[j], out_dtype=BF16, tm=tm, tn=512, head_gain=sb_q_norm[j],
                           normed_cols=sb_heads * SB_HEAD_DIM, scale=SB_HEAD_DIM ** -0.5, name="q_proj")
            o = _sb_attention(q.reshape(batch, seq, -1), kv_sh, batch=batch, seq=seq, heads=sb_heads,
                              tile=SB_TILE, sub=SB_SUB)
            h = _proj_res(o.reshape(t, -1), w_o[j], h, tm=tm, tn=1024, name="sb_out")
        h = _ffn(h, ffn_norm[i, 1], wg_all[i, 1], wu_all[i, 1], wd_all[i, 1], tm=tm, tf=2 * MXU_DIM)
        h = _ple(h, p2[i], ple_norm[i], w_pp[i], w_pg[i], tm=tm, tn=1024)
    return h.reshape(batch, seq, d)
```

```python
import functools

import jax
import jax.numpy as jnp
from jax import lax
from jax.experimental import pallas as pl
from jax.experimental.pallas import tpu as pltpu

F32 = jnp.float32
BF16 = jnp.bfloat16

RMS_EPS = 1e-6
FORGET_HEADS = 4
SB_HEAD_DIM = 128
LANES = 128
MXU_DIM = 256
VMEM_LIMIT = 56 * 1024 * 1024

ROW_TILE = 1024
ROW_TILE_RESIDENT = 512
COL_CHUNK = 512
FFN_COLS = 2 * MXU_DIM
NORM_ROWS = 256
MLSTM_CHUNK = 256
SB_TILE = 256
SB_SUB = 4
F32_EXP_ZERO = -104.0


def _params(*sem):
    return pltpu.CompilerParams(dimension_semantics=sem, vmem_limit_bytes=VMEM_LIMIT)


def _sigmoid(x):
    return 1.0 / (1.0 + jnp.exp(-x))


def _softplus_neg_abs(x):
    return jnp.log1p(jnp.exp(-jnp.abs(x)))


def _rmsnorm_rows(x_ref, g_ref, xn_ref):
    rows = x_ref.shape[0]
    step = min(NORM_ROWS, rows)

    def body(r, carry):
        sl = pl.ds(pl.multiple_of(r * step, step), step)
        x = x_ref[sl, :]
        ms = jnp.mean(x * x, axis=-1, keepdims=True)
        xn_ref[sl, :] = (x * lax.rsqrt(ms + RMS_EPS) * g_ref[...]).astype(BF16)
        return carry

    lax.fori_loop(0, rows // step, body, 0)


def _col_chunks(n):
    step = min(COL_CHUNK, n)
    return [slice(c, c + step) for c in range(0, n, step)]


def _ffn_kernel(x_ref, g_ref, wg_ref, wu_ref, wd_ref, o_ref, xn_ref, *, last_cols):
    j = pl.program_id(1)
    last = pl.num_programs(1) - 1
    tf = wg_ref.shape[1]

    @pl.when(j == 0)
    def _():
        _rmsnorm_rows(x_ref, g_ref, xn_ref)
        o_ref[...] = x_ref[...]

    def accumulate(cols):
        xn = xn_ref[...]
        gate = jnp.dot(xn, wg_ref[:, :cols], preferred_element_type=F32)
        up = jnp.dot(xn, wu_ref[:, :cols], preferred_element_type=F32)
        hid = (gate * _sigmoid(gate) * up * 0.5).astype(BF16)
        for cs in _col_chunks(o_ref.shape[1]):
            o_ref[:, cs] += jnp.dot(hid, wd_ref[:cols, cs], preferred_element_type=F32)

    if last_cols == tf:
        accumulate(tf)
    else:
        pl.when(j < last)(lambda: accumulate(tf))
        pl.when(j == last)(lambda: accumulate(last_cols))


def _ffn(h, g, wg, wu, wd, *, tm, tf):
    t, d = h.shape
    ff = wg.shape[1]
    steps = pl.cdiv(ff, tf)
    return pl.pallas_call(
        functools.partial(_ffn_kernel, last_cols=ff - (steps - 1) * tf),
        grid=(t // tm, steps),
        in_specs=[
            pl.BlockSpec((tm, d), lambda i, j: (i, 0)),
            pl.BlockSpec((1, d), lambda i, j: (0, 0)),
            pl.BlockSpec((d, tf), lambda i, j: (0, j)),
            pl.BlockSpec((d, tf), lambda i, j: (0, j)),
            pl.BlockSpec((tf, d), lambda i, j: (j, 0)),
        ],
        out_specs=pl.BlockSpec((tm, d), lambda i, j: (i, 0)),
        out_shape=jax.ShapeDtypeStruct((t, d), F32),
        scratch_shapes=[pltpu.VMEM((tm, d), BF16)],
        compiler_params=_params("parallel", "arbitrary"),
        name="ffn",
    )(h, g.reshape(1, d), wg, wu, wd)


def _head_norm(y, gain_row, scale):
    outs = []
    for s in range(y.shape[1] // SB_HEAD_DIM):
        ys = y[:, s * SB_HEAD_DIM:(s + 1) * SB_HEAD_DIM]
        ms = jnp.mean(ys * ys, axis=-1, keepdims=True)
        outs.append(ys * lax.rsqrt(ms + RMS_EPS) * gain_row * scale)
    return jnp.concatenate(outs, axis=1) if len(outs) > 1 else outs[0]


def _norm_proj_kernel(x_ref, g_ref, w_ref, hg_ref, o_ref, xn_ref, *, normed_blocks, scale):
    j = pl.program_id(1)

    @pl.when(j == 0)
    def _():
        _rmsnorm_rows(x_ref, g_ref, xn_ref)

    y = jnp.dot(xn_ref[...], w_ref[...], preferred_element_type=F32)

    @pl.when(j < normed_blocks)
    def _():
        o_ref[...] = _head_norm(y, hg_ref[...], scale).astype(o_ref.dtype)

    @pl.when(j >= normed_blocks)
    def _():
        o_ref[...] = y.astype(o_ref.dtype)


def _norm_proj(h, g, w, head_gain, *, normed_cols, tm, tn, name):
    t, d = h.shape
    n = w.shape[1]
    return pl.pallas_call(
        functools.partial(_norm_proj_kernel, normed_blocks=normed_cols // tn, scale=1.0),
        grid=(t // tm, n // tn),
        in_specs=[
            pl.BlockSpec((tm, d), lambda i, j: (i, 0)),
            pl.BlockSpec((1, d), lambda i, j: (0, 0)),
            pl.BlockSpec((d, tn), lambda i, j: (0, j)),
            pl.BlockSpec((1, SB_HEAD_DIM), lambda i, j: (0, 0)),
        ],
        out_specs=pl.BlockSpec((tm, tn), lambda i, j: (i, j)),
        out_shape=jax.ShapeDtypeStruct((t, n), BF16),
        scratch_shapes=[pltpu.VMEM((tm, d), BF16)],
        compiler_params=_params("parallel", "arbitrary"),
        name=name,
    )(h, g.reshape(1, d), w, head_gain.reshape(1, SB_HEAD_DIM).astype(F32))


def _norm_proj_resident_kernel(x_ref, g_ref, w_ref, hg_ref, o_ref, xn_ref, *, scale):
    _rmsnorm_rows(x_ref, g_ref, xn_ref)
    xn = xn_ref[...]
    for cs in _col_chunks(o_ref.shape[1]):
        y = jnp.dot(xn, w_ref[:, cs], preferred_element_type=F32)
        o_ref[:, cs] = _head_norm(y, hg_ref[...], scale).astype(o_ref.dtype)


def _norm_proj_resident(h, g, w, head_gain, *, scale, tm, name):
    t, d = h.shape
    n = w.shape[1]
    return pl.pallas_call(
        functools.partial(_norm_proj_resident_kernel, scale=scale),
        grid=(t // tm,),
        in_specs=[
            pl.BlockSpec((tm, d), lambda i: (i, 0)),
            pl.BlockSpec((1, d), lambda i: (0, 0)),
            pl.BlockSpec((d, n), lambda i: (0, 0)),
            pl.BlockSpec((1, SB_HEAD_DIM), lambda i: (0, 0)),
        ],
        out_specs=pl.BlockSpec((tm, n), lambda i: (i, 0)),
        out_shape=jax.ShapeDtypeStruct((t, n), BF16),
        scratch_shapes=[pltpu.VMEM((tm, d), BF16)],
        compiler_params=_params("parallel"),
        name=name,
    )(h, g.reshape(1, d), w, head_gain.reshape(1, SB_HEAD_DIM).astype(F32))


def _mlstm_in_kernel(x_ref, g_ref, w_ref, wgate_ref, o_ref, gates_ref, xn_ref):
    j = pl.program_id(1)

    @pl.when(j == 0)
    def _():
        _rmsnorm_rows(x_ref, g_ref, xn_ref)
        gates_ref[...] = jnp.dot(xn_ref[...], wgate_ref[...], preferred_element_type=F32)

    o_ref[...] = jnp.dot(xn_ref[...], w_ref[...], preferred_element_type=F32).astype(o_ref.dtype)


def _mlstm_in(h, g, w, w_gates, *, tm, tn):
    t, d = h.shape
    n = w.shape[1]
    return pl.pallas_call(
        _mlstm_in_kernel,
        grid=(t // tm, n // tn),
        in_specs=[
            pl.BlockSpec((tm, d), lambda i, j: (i, 0)),
            pl.BlockSpec((1, d), lambda i, j: (0, 0)),
            pl.BlockSpec((d, tn), lambda i, j: (0, j)),
            pl.BlockSpec((d, LANES), lambda i, j: (0, 0)),
        ],
        out_specs=[pl.BlockSpec((tm, tn), lambda i, j: (i, j)),
                   pl.BlockSpec((tm, LANES), lambda i, j: (i, 0))],
        out_shape=[jax.ShapeDtypeStruct((t, n), BF16), jax.ShapeDtypeStruct((t, LANES), F32)],
        scratch_shapes=[pltpu.VMEM((tm, d), BF16)],
        compiler_params=_params("parallel", "arbitrary"),
        name="mlstm_in",
    )(h, g.reshape(1, d), w, w_gates)


def _proj_res_kernel(a_ref, w_ref, r_ref, o_ref):
    a = a_ref[...]
    for cs in _col_chunks(o_ref.shape[1]):
        o_ref[:, cs] = r_ref[:, cs] + jnp.dot(a, w_ref[:, cs], preferred_element_type=F32)


def _proj_res(a, w, res, *, tm, name):
    t, k = a.shape
    n = w.shape[1]
    return pl.pallas_call(
        _proj_res_kernel,
        grid=(t // tm,),
        in_specs=[
            pl.BlockSpec((tm, k), lambda i: (i, 0)),
            pl.BlockSpec((k, n), lambda i: (0, 0)),
            pl.BlockSpec((tm, n), lambda i: (i, 0)),
        ],
        out_specs=pl.BlockSpec((tm, n), lambda i: (i, 0)),
        out_shape=jax.ShapeDtypeStruct((t, n), F32),
        compiler_params=_params("parallel"),
        name=name,
    )(a, w, res)


def _ple_kernel(x_ref, g_ref, wg_ref, p_ref, wp_ref, o_ref, xn_ref):
    _rmsnorm_rows(x_ref, g_ref, xn_ref)
    xn = xn_ref[...]
    pb = p_ref[...].astype(BF16)
    for cs in _col_chunks(o_ref.shape[1]):
        gate = _sigmoid(jnp.dot(xn, wg_ref[:, cs], preferred_element_type=F32))
        emb = jnp.dot(pb, wp_ref[:, cs], preferred_element_type=F32)
        o_ref[:, cs] = x_ref[:, cs] + emb * gate


def _ple(h, p_i, g, w_proj, w_gate, *, tm):
    t, d = h.shape
    dp = p_i.shape[1]
    return pl.pallas_call(
        _ple_kernel,
        grid=(t // tm,),
        in_specs=[
            pl.BlockSpec((tm, d), lambda i: (i, 0)),
            pl.BlockSpec((1, d), lambda i: (0, 0)),
            pl.BlockSpec((d, d), lambda i: (0, 0)),
            pl.BlockSpec((tm, dp), lambda i: (i, 0)),
            pl.BlockSpec((dp, d), lambda i: (0, 0)),
        ],
        out_specs=pl.BlockSpec((tm, d), lambda i: (i, 0)),
        out_shape=jax.ShapeDtypeStruct((t, d), F32),
        scratch_shapes=[pltpu.VMEM((tm, d), BF16)],
        compiler_params=_params("parallel"),
        name="ple",
    )(h, g.reshape(1, d), w_gate, p_i, w_proj)


def _mlstm_kernel(q_ref, k_ref, v_ref, og_ref, gates_ref, bias_ref, hg_ref, o_ref,
                  c_ref, n_ref, m_ref, *, dk):
    head = pl.program_id(1)
    chunk = pl.program_id(2)
    L = q_ref.shape[0]

    @pl.when(chunk == 0)
    def _():
        c_ref[...] = jnp.zeros_like(c_ref)
        n_ref[...] = jnp.zeros_like(n_ref)
        m_ref[...] = jnp.zeros_like(m_ref)

    gates = gates_ref[...] + bias_ref[...]
    lane = lax.broadcasted_iota(jnp.int32, gates.shape, 1)
    gi = jnp.sum(jnp.where(lane == head, gates, 0.0), axis=1, keepdims=True)
    gf = jnp.sum(jnp.where(lane == head + FORGET_HEADS, gates, 0.0), axis=1, keepdims=True)
    lf = jnp.minimum(gf, 0.0) - _softplus_neg_abs(gf)

    row = lax.broadcasted_iota(jnp.int32, (L, L), 0)
    col = lax.broadcasted_iota(jnp.int32, (L, L), 1)
    eye = row == col
    causal = col <= row
    lf_b = jnp.broadcast_to(lf, (L, L))
    gi_b = jnp.broadcast_to(gi, (L, L))
    lf_row = jnp.sum(jnp.where(eye, lf_b, 0.0), axis=0, keepdims=True)
    gi_row = jnp.sum(jnp.where(eye, gi_b, 0.0), axis=0, keepdims=True)
    b_row = jnp.sum(jnp.where(row <= col, lf_b, 0.0), axis=0, keepdims=True)
    b_col = jnp.sum(jnp.where(causal, jnp.broadcast_to(lf_row, (L, L)), 0.0), axis=1, keepdims=True)

    m_prev = m_ref[...]
    a_col = b_col + m_prev
    dm = jnp.where(causal, b_col - b_row + gi_row, -jnp.inf)
    m_t = jnp.maximum(a_col, jnp.max(dm, axis=1, keepdims=True))
    w_inter = jnp.exp(a_col - m_t)
    decay_mat = jnp.exp(dm - m_t)

    q = q_ref[...] * (dk ** -0.5)
    k = k_ref[...]
    v = v_ref[...]
    s_qk = lax.dot_general(q, k, (((1,), (1,)), ((), ())), preferred_element_type=F32) * decay_mat
    c_prev = c_ref[...]
    num = w_inter * jnp.dot(q, c_prev.astype(BF16), preferred_element_type=F32) \
        + jnp.dot(s_qk.astype(BF16), v, preferred_element_type=F32)
    qn = jnp.sum(q.astype(F32) * n_ref[...], axis=1, keepdims=True)
    den = w_inter * qn + jnp.sum(s_qk, axis=1, keepdims=True)
    hs = num / jnp.maximum(jnp.abs(den), jnp.exp(-m_t))

    ms = jnp.mean(hs * hs, axis=-1, keepdims=True)
    hs = hs * lax.rsqrt(ms + RMS_EPS) * hg_ref[...]
    o_ref[...] = (hs * _sigmoid(og_ref[...].astype(F32))).astype(o_ref.dtype)

    b_last = b_col[L - 1:L, :]
    g_end = b_last - b_col + gi
    m_new = jnp.maximum(b_last + m_prev, jnp.max(g_end, axis=0, keepdims=True))
    decay = jnp.exp(b_last + m_prev - m_new)
    kw = k.astype(F32) * jnp.exp(g_end - m_new)
    c_ref[...] = decay * c_prev + lax.dot_general(
        kw.astype(BF16), v, (((0,), (0,)), ((), ())), preferred_element_type=F32)
    n_ref[...] = decay * n_ref[...] + jnp.sum(kw, axis=0, keepdims=True)
    m_ref[...] = m_new


def _mlstm_cell(proj, gates, bias_row, head_gain, *, batch, seq, heads, dk, dv, chunk):
    t = proj.shape[0]
    nc = seq // chunk
    kq = heads
    vq = 2 * heads * dk // dv
    oq = vq + heads
    row = lambda b, h, c: b * nc + c
    return pl.pallas_call(
        functools.partial(_mlstm_kernel, dk=dk),
        grid=(batch, heads, nc),
        in_specs=[
            pl.BlockSpec((chunk, dk), lambda b, h, c: (row(b, h, c), h)),
            pl.BlockSpec((chunk, dk), lambda b, h, c: (row(b, h, c), kq + h)),
            pl.BlockSpec((chunk, dv), lambda b, h, c: (row(b, h, c), vq + h)),
            pl.BlockSpec((chunk, dv), lambda b, h, c: (row(b, h, c), oq + h)),
            pl.BlockSpec((chunk, LANES), lambda b, h, c: (row(b, h, c), 0)),
            pl.BlockSpec((1, LANES), lambda b, h, c: (0, 0)),
            pl.BlockSpec((1, dv), lambda b, h, c: (0, h)),
        ],
        out_specs=pl.BlockSpec((chunk, dv), lambda b, h, c: (row(b, h, c), h)),
        out_shape=jax.ShapeDtypeStruct((t, heads * dv), BF16),
        scratch_shapes=[pltpu.VMEM((dk, dv), F32), pltpu.VMEM((1, dk), F32), pltpu.VMEM((1, 1), F32)],
        compiler_params=_params("parallel", "parallel", "arbitrary"),
        name="mlstm_cell",
    )(proj, proj, proj, proj, gates, bias_row, head_gain.reshape(1, heads * dv))


def _sb_kernel(q_ref, k_ref, v_ref, o_ref, *, tile, sub):
    i = pl.program_id(2)
    d = q_ref.shape[1]
    row = lax.broadcasted_iota(jnp.int32, (tile, tile), 0)
    col = lax.broadcasted_iota(jnp.int32, (tile, tile), 1)
    later = (row > col).astype(BF16)
    later2 = jnp.concatenate([later, later], axis=0)
    strict = col < row
    qs = [q_ref[r * tile:(r + 1) * tile, :] for r in range(sub)]

    def sweep_tile(q, kb, c, acc, diagonal):
        start = pl.multiple_of(kb * tile, tile)
        k = k_ref[pl.ds(start, tile), :]
        v = v_ref[pl.ds(start, tile), :]
        z = lax.dot_general(q, k, (((1,), (1,)), ((), ())), preferred_element_type=F32)
        sp = jnp.log(1.0 + jnp.exp(-jnp.abs(z)))
        log_beta = jnp.minimum(z, 0.0) - sp
        log_keep = -jnp.maximum(z, 0.0) - sp
        if diagonal:
            log_keep = jnp.where(strict, log_keep, 0.0)
        hi = log_keep.astype(BF16)
        lo = (log_keep - hi.astype(F32)).astype(BF16)
        after = jnp.dot(jnp.concatenate([hi, lo], axis=1), later2, preferred_element_type=F32)
        a = jnp.exp(log_beta + after + c)
        if diagonal:
            a = jnp.where(strict, a, 0.0)
        acc = acc + jnp.dot(a.astype(BF16), v, preferred_element_type=F32)
        c = c + after[:, 0:1] + log_keep[:, 0:1]
        return jnp.where(kb <= 0, -jnp.inf, c), acc

    def unfinished(cs):
        worst = functools.reduce(jnp.maximum, [jnp.max(c, axis=0, keepdims=True) for c in cs])
        return jnp.max(worst) >= F32_EXP_ZERO

    cs, accs = [], []
    for r in range(sub):
        c, acc = sweep_tile(qs[r], i * sub + r, jnp.zeros((tile, 1), F32), jnp.zeros((tile, d), F32), True)
        cs.append(c)
        accs.append(acc)

    def body(carry):
        n, _, cs, accs = carry
        new_cs, new_accs = [], []
        for r in range(sub):
            kb = jnp.maximum(i * sub + r - n, 0)
            c, acc = sweep_tile(qs[r], kb, cs[r], accs[r], False)
            new_cs.append(c)
            new_accs.append(acc)
        return n + 1, unfinished(new_cs), tuple(new_cs), tuple(new_accs)

    init = (jnp.int32(1), unfinished(cs), tuple(cs), tuple(accs))
    _, _, _, accs = lax.while_loop(lambda carry: carry[1], body, init)
    for r in range(sub):
        o_ref[r * tile:(r + 1) * tile, :] = accs[r].astype(o_ref.dtype)


def _sb_attention(q, kv, *, batch, seq, heads, tile, sub):
    d = SB_HEAD_DIM
    tq = tile * sub
    return pl.pallas_call(
        functools.partial(_sb_kernel, tile=tile, sub=sub),
        grid=(batch, heads, seq // tq),
        in_specs=[
            pl.BlockSpec((None, tq, d), lambda b, h, i: (b, i, h)),
            pl.BlockSpec((None, seq, d), lambda b, h, i: (b, 0, h)),
            pl.BlockSpec((None, seq, d), lambda b, h, i: (b, 0, heads + h)),
        ],
        out_specs=pl.BlockSpec((None, tq, d), lambda b, h, i: (b, i, h)),
        out_shape=jax.ShapeDtypeStruct((batch, seq, heads * d), BF16),
        compiler_params=_params("parallel", "parallel", "arbitrary"),
        name="sb_attention",
    )(q, kv, kv)


def kernel(x, p, ffn_norm, ffn_w_gate, ffn_w_up, ffn_w_down, mix_norm, mlstm_w_in, mlstm_b_if,
           mlstm_head_norm, mlstm_w_out, kv_norm, sb_w_kv, sb_k_norm, sb_w_q, sb_q_norm, sb_w_o,
           ple_norm, ple_w_proj, ple_w_gate):
    batch, seq, d = x.shape
    depth = p.shape[0]
    n_a = mlstm_w_in.shape[0]
    t = batch * seq
    heads = FORGET_HEADS
    dv = mlstm_w_out.shape[1] // heads
    dk = (mlstm_w_in.shape[2] - 2 * heads * dv - 2 * heads) // (2 * heads)
    n_main = 2 * heads * dk + 2 * heads * dv
    sb_heads = sb_w_q.shape[2] // SB_HEAD_DIM
    tm, tmr = ROW_TILE, ROW_TILE_RESIDENT

    wg_all = ffn_w_gate.astype(BF16)
    wu_all = ffn_w_up.astype(BF16)
    wd_all = ffn_w_down.astype(BF16)
    w_in_main = mlstm_w_in[:, :, :n_main].astype(BF16)
    w_in_gates = jnp.pad(mlstm_w_in[:, :, n_main:].astype(BF16), ((0, 0), (0, 0), (0, LANES - 2 * heads)))
    bias_rows = jnp.pad(mlstm_b_if.astype(F32), ((0, 0), (0, LANES - 2 * heads)))
    w_out = mlstm_w_out.astype(BF16)
    w_kv = sb_w_kv.astype(BF16)
    w_q = sb_w_q.astype(BF16)
    w_o = sb_w_o.astype(BF16)
    w_pp = ple_w_proj.astype(BF16)
    w_pg = ple_w_gate.astype(BF16)

    h = x.reshape(t, d)
    p2 = p.reshape(depth, t, p.shape[-1])
    kv_sh = None
    for i in range(depth):
        if i == n_a:
            kv_sh = _norm_proj(h, kv_norm, w_kv, sb_k_norm, normed_cols=sb_heads * SB_HEAD_DIM,
                               tm=tm, tn=2 * COL_CHUNK, name="kv_proj").reshape(batch, seq, -1)
        h = _ffn(h, ffn_norm[i, 0], wg_all[i, 0], wu_all[i, 0], wd_all[i, 0], tm=tm, tf=FFN_COLS)
        if i < n_a:
            proj, gates = _mlstm_in(h, mix_norm[i], w_in_main[i], w_in_gates[i], tm=tm, tn=2 * COL_CHUNK)
            hs = _mlstm_cell(proj, gates, bias_rows[i:i + 1], mlstm_head_norm[i], batch=batch, seq=seq,
                             heads=heads, dk=dk, dv=dv, chunk=MLSTM_CHUNK)
            h = _proj_res(hs, w_out[i], h, tm=tmr, name="mlstm_out")
        else:
            j = i - n_a
            q = _norm_proj_resident(h, mix_norm[i], w_q[j], sb_q_norm[j], scale=SB_HEAD_DIM ** -0.5,
                                    tm=tm, name="q_proj")
            o = _sb_attention(q.reshape(batch, seq, -1), kv_sh, batch=batch, seq=seq, heads=sb_heads,
                              tile=SB_TILE, sub=SB_SUB)
            h = _proj_res(o.reshape(t, -1), w_o[j], h, tm=tmr, name="sb_out")
        h = _ffn(h, ffn_norm[i, 1], wg_all[i, 1], wu_all[i, 1], wd_all[i, 1], tm=tm, tf=FFN_COLS)
        h = _ple(h, p2[i], ple_norm[i], w_pp[i], w_pg[i], tm=tmr)
    return h.reshape(batch, seq, d)
```

```python
import functools

import jax
import jax.numpy as jnp
from jax import lax
from jax.experimental import pallas as pl
from jax.experimental.pallas import tpu as pltpu

F32 = jnp.float32
BF16 = jnp.bfloat16

RMS_EPS = 1e-6
FORGET_HEADS = 4
SB_HEAD_DIM = 128
LANES = 128
MXU_DIM = 256
VMEM_LIMIT = 56 * 1024 * 1024

ROW_TILE = 1024
ROW_TILE_RESIDENT = 512
COL_CHUNK = 512
FFN_COLS = 2 * MXU_DIM
NORM_ROWS = 256
MLSTM_CHUNK = 256
SB_TILE = 256
SB_SUB = 8
F32_EXP_ZERO = -104.0
LOG2_E = 1.4426950408889634


def _params(*sem):
    return pltpu.CompilerParams(dimension_semantics=sem, vmem_limit_bytes=VMEM_LIMIT)


def _sigmoid(x):
    return 1.0 / (1.0 + jnp.exp(-x))


def _softplus_neg_abs(x):
    return jnp.log1p(jnp.exp(-jnp.abs(x)))


def _rmsnorm_rows(x_ref, g_ref, xn_ref):
    rows = x_ref.shape[0]
    step = min(NORM_ROWS, rows)

    def body(r, carry):
        sl = pl.ds(pl.multiple_of(r * step, step), step)
        x = x_ref[sl, :]
        ms = jnp.mean(x * x, axis=-1, keepdims=True)
        xn_ref[sl, :] = (x * lax.rsqrt(ms + RMS_EPS) * g_ref[...]).astype(BF16)
        return carry

    lax.fori_loop(0, rows // step, body, 0)


def _col_chunks(n):
    step = min(COL_CHUNK, n)
    return [slice(c, c + step) for c in range(0, n, step)]


def _ffn_kernel(x_ref, g_ref, wg_ref, wu_ref, wd_ref, o_ref, xn_ref, *, last_cols):
    j = pl.program_id(1)
    last = pl.num_programs(1) - 1
    tf = wg_ref.shape[1]

    @pl.when(j == 0)
    def _():
        _rmsnorm_rows(x_ref, g_ref, xn_ref)
        o_ref[...] = x_ref[...]

    def accumulate(cols):
        xn = xn_ref[...]
        gate = jnp.dot(xn, wg_ref[:, :cols], preferred_element_type=F32)
        up = jnp.dot(xn, wu_ref[:, :cols], preferred_element_type=F32)
        hid = (gate * _sigmoid(gate) * up * 0.5).astype(BF16)
        for cs in _col_chunks(o_ref.shape[1]):
            o_ref[:, cs] += jnp.dot(hid, wd_ref[:cols, cs], preferred_element_type=F32)

    if last_cols == tf:
        accumulate(tf)
    else:
        pl.when(j < last)(lambda: accumulate(tf))
        pl.when(j == last)(lambda: accumulate(last_cols))


def _ffn(h, g, wg, wu, wd, *, layer, half, tm, tf):
    t, d = h.shape
    ff = wg.shape[-1]
    steps = pl.cdiv(ff, tf)
    return pl.pallas_call(
        functools.partial(_ffn_kernel, last_cols=ff - (steps - 1) * tf),
        grid=(t // tm, steps),
        in_specs=[
            pl.BlockSpec((tm, d), lambda i, j: (i, 0)),
            pl.BlockSpec((1, d), lambda i, j: (0, 0)),
            pl.BlockSpec((None, None, d, tf), lambda i, j: (layer, half, 0, j)),
            pl.BlockSpec((None, None, d, tf), lambda i, j: (layer, half, 0, j)),
            pl.BlockSpec((None, None, tf, d), lambda i, j: (layer, half, j, 0)),
        ],
        out_specs=pl.BlockSpec((tm, d), lambda i, j: (i, 0)),
        out_shape=jax.ShapeDtypeStruct((t, d), F32),
        scratch_shapes=[pltpu.VMEM((tm, d), BF16)],
        compiler_params=_params("parallel", "arbitrary"),
        name="ffn",
    )(h, g.reshape(1, d), wg, wu, wd)


def _head_norm(y, gain_row, scale):
    outs = []
    for s in range(y.shape[1] // SB_HEAD_DIM):
        ys = y[:, s * SB_HEAD_DIM:(s + 1) * SB_HEAD_DIM]
        ms = jnp.mean(ys * ys, axis=-1, keepdims=True)
        outs.append(ys * lax.rsqrt(ms + RMS_EPS) * gain_row * scale)
    return jnp.concatenate(outs, axis=1) if len(outs) > 1 else outs[0]


def _norm_proj_kernel(x_ref, g_ref, w_ref, hg_ref, o_ref, xn_ref, *, normed_blocks, scale):
    j = pl.program_id(1)

    @pl.when(j == 0)
    def _():
        _rmsnorm_rows(x_ref, g_ref, xn_ref)

    y = jnp.dot(xn_ref[...], w_ref[...], preferred_element_type=F32)

    @pl.when(j < normed_blocks)
    def _():
        o_ref[...] = _head_norm(y, hg_ref[...], scale).astype(o_ref.dtype)

    @pl.when(j >= normed_blocks)
    def _():
        o_ref[...] = y.astype(o_ref.dtype)


def _norm_proj(h, g, w, head_gain, *, normed_cols, tm, tn, name):
    t, d = h.shape
    n = w.shape[1]
    return pl.pallas_call(
        functools.partial(_norm_proj_kernel, normed_blocks=normed_cols // tn, scale=1.0),
        grid=(t // tm, n // tn),
        in_specs=[
            pl.BlockSpec((tm, d), lambda i, j: (i, 0)),
            pl.BlockSpec((1, d), lambda i, j: (0, 0)),
            pl.BlockSpec((d, tn), lambda i, j: (0, j)),
            pl.BlockSpec((1, SB_HEAD_DIM), lambda i, j: (0, 0)),
        ],
        out_specs=pl.BlockSpec((tm, tn), lambda i, j: (i, j)),
        out_shape=jax.ShapeDtypeStruct((t, n), BF16),
        scratch_shapes=[pltpu.VMEM((tm, d), BF16)],
        compiler_params=_params("parallel", "arbitrary"),
        name=name,
    )(h, g.reshape(1, d), w, head_gain.reshape(1, SB_HEAD_DIM).astype(F32))


def _norm_proj_resident_kernel(x_ref, g_ref, w_ref, hg_ref, o_ref, xn_ref, *, scale):
    _rmsnorm_rows(x_ref, g_ref, xn_ref)
    xn = xn_ref[...]
    for cs in _col_chunks(o_ref.shape[1]):
        y = jnp.dot(xn, w_ref[:, cs], preferred_element_type=F32)
        o_ref[:, cs] = _head_norm(y, hg_ref[...], scale).astype(o_ref.dtype)


def _norm_proj_resident(h, g, w, head_gain, *, layer, scale, tm, name):
    t, d = h.shape
    n = w.shape[2]
    return pl.pallas_call(
        functools.partial(_norm_proj_resident_kernel, scale=scale),
        grid=(t // tm,),
        in_specs=[
            pl.BlockSpec((tm, d), lambda i: (i, 0)),
            pl.BlockSpec((1, d), lambda i: (0, 0)),
            pl.BlockSpec((None, d, n), lambda i: (layer, 0, 0)),
            pl.BlockSpec((1, SB_HEAD_DIM), lambda i: (0, 0)),
        ],
        out_specs=pl.BlockSpec((tm, n), lambda i: (i, 0)),
        out_shape=jax.ShapeDtypeStruct((t, n), BF16),
        scratch_shapes=[pltpu.VMEM((tm, d), BF16)],
        compiler_params=_params("parallel"),
        name=name,
    )(h, g.reshape(1, d), w, head_gain.reshape(1, SB_HEAD_DIM).astype(F32))


def _mlstm_in_kernel(x_ref, g_ref, w_ref, wgate_ref, o_ref, gates_ref, xn_ref):
    j = pl.program_id(1)

    @pl.when(j == 0)
    def _():
        _rmsnorm_rows(x_ref, g_ref, xn_ref)
        gates_ref[...] = jnp.dot(xn_ref[...], wgate_ref[...], preferred_element_type=F32)

    o_ref[...] = jnp.dot(xn_ref[...], w_ref[...], preferred_element_type=F32).astype(o_ref.dtype)


def _mlstm_in(h, g, w, w_gates, *, layer, n, tm, tn):
    t, d = h.shape
    return pl.pallas_call(
        _mlstm_in_kernel,
        grid=(t // tm, n // tn),
        in_specs=[
            pl.BlockSpec((tm, d), lambda i, j: (i, 0)),
            pl.BlockSpec((1, d), lambda i, j: (0, 0)),
            pl.BlockSpec((None, d, tn), lambda i, j: (layer, 0, j)),
            pl.BlockSpec((d, LANES), lambda i, j: (0, 0)),
        ],
        out_specs=[pl.BlockSpec((tm, tn), lambda i, j: (i, j)),
                   pl.BlockSpec((tm, LANES), lambda i, j: (i, 0))],
        out_shape=[jax.ShapeDtypeStruct((t, n), BF16), jax.ShapeDtypeStruct((t, LANES), F32)],
        scratch_shapes=[pltpu.VMEM((tm, d), BF16)],
        compiler_params=_params("parallel", "arbitrary"),
        name="mlstm_in",
    )(h, g.reshape(1, d), w, w_gates)


def _proj_res_kernel(a_ref, w_ref, r_ref, o_ref):
    a = a_ref[...]
    for cs in _col_chunks(o_ref.shape[1]):
        o_ref[:, cs] = r_ref[:, cs] + jnp.dot(a, w_ref[:, cs], preferred_element_type=F32)


def _proj_res(a, w, res, *, layer, tm, name):
    t, k = a.shape
    n = w.shape[2]
    return pl.pallas_call(
        _proj_res_kernel,
        grid=(t // tm,),
        in_specs=[
            pl.BlockSpec((tm, k), lambda i: (i, 0)),
            pl.BlockSpec((None, k, n), lambda i: (layer, 0, 0)),
            pl.BlockSpec((tm, n), lambda i: (i, 0)),
        ],
        out_specs=pl.BlockSpec((tm, n), lambda i: (i, 0)),
        out_shape=jax.ShapeDtypeStruct((t, n), F32),
        compiler_params=_params("parallel"),
        name=name,
    )(a, w, res)


def _ple_kernel(x_ref, g_ref, wg_ref, p_ref, wp_ref, o_ref, xn_ref):
    _rmsnorm_rows(x_ref, g_ref, xn_ref)
    xn = xn_ref[...]
    pb = p_ref[...].astype(BF16)
    for cs in _col_chunks(o_ref.shape[1]):
        gate = _sigmoid(jnp.dot(xn, wg_ref[:, cs], preferred_element_type=F32))
        emb = jnp.dot(pb, wp_ref[:, cs], preferred_element_type=F32)
        o_ref[:, cs] = x_ref[:, cs] + emb * gate


def _ple(h, p, g, w_proj, w_gate, *, layer, tm):
    t, d = h.shape
    dp = p.shape[2]
    return pl.pallas_call(
        _ple_kernel,
        grid=(t // tm,),
        in_specs=[
            pl.BlockSpec((tm, d), lambda i: (i, 0)),
            pl.BlockSpec((1, d), lambda i: (0, 0)),
            pl.BlockSpec((None, d, d), lambda i: (layer, 0, 0)),
            pl.BlockSpec((None, tm, dp), lambda i: (layer, i, 0)),
            pl.BlockSpec((None, dp, d), lambda i: (layer, 0, 0)),
        ],
        out_specs=pl.BlockSpec((tm, d), lambda i: (i, 0)),
        out_shape=jax.ShapeDtypeStruct((t, d), F32),
        scratch_shapes=[pltpu.VMEM((tm, d), BF16)],
        compiler_params=_params("parallel"),
        name="ple",
    )(h, g.reshape(1, d), w_gate, p, w_proj)


def _mlstm_kernel(q_ref, k_ref, v_ref, og_ref, gates_ref, bias_ref, hg_ref, o_ref,
                  c_ref, n_ref, m_ref, *, dk):
    head = pl.program_id(1)
    chunk = pl.program_id(2)
    L = q_ref.shape[0]

    @pl.when(chunk == 0)
    def _():
        c_ref[...] = jnp.zeros_like(c_ref)
        n_ref[...] = jnp.zeros_like(n_ref)
        m_ref[...] = jnp.zeros_like(m_ref)

    gates = gates_ref[...] + bias_ref[...]
    lane = lax.broadcasted_iota(jnp.int32, gates.shape, 1)
    gi = jnp.sum(jnp.where(lane == head, gates, 0.0), axis=1, keepdims=True)
    gf = jnp.sum(jnp.where(lane == head + FORGET_HEADS, gates, 0.0), axis=1, keepdims=True)
    lf = jnp.minimum(gf, 0.0) - _softplus_neg_abs(gf)

    row = lax.broadcasted_iota(jnp.int32, (L, L), 0)
    col = lax.broadcasted_iota(jnp.int32, (L, L), 1)
    eye = row == col
    causal = col <= row
    lf_b = jnp.broadcast_to(lf, (L, L))
    gi_b = jnp.broadcast_to(gi, (L, L))
    lf_row = jnp.sum(jnp.where(eye, lf_b, 0.0), axis=0, keepdims=True)
    gi_row = jnp.sum(jnp.where(eye, gi_b, 0.0), axis=0, keepdims=True)
    b_row = jnp.sum(jnp.where(row <= col, lf_b, 0.0), axis=0, keepdims=True)
    b_col = jnp.sum(jnp.where(causal, jnp.broadcast_to(lf_row, (L, L)), 0.0), axis=1, keepdims=True)

    m_prev = m_ref[...]
    a_col = b_col + m_prev
    dm = jnp.where(causal, b_col - b_row + gi_row, -jnp.inf)
    m_t = jnp.maximum(a_col, jnp.max(dm, axis=1, keepdims=True))
    w_inter = jnp.exp(a_col - m_t)
    decay_mat = jnp.exp(dm - m_t)

    q = q_ref[...] * (dk ** -0.5)
    k = k_ref[...]
    v = v_ref[...]
    c_prev = c_ref[...]
    qk = lax.dot_general(q, k, (((1,), (1,)), ((), ())), preferred_element_type=F32)
    qc = jnp.dot(q, c_prev.astype(BF16), preferred_element_type=F32)

    b_last = b_col[L - 1:L, :]
    g_end = b_last - b_col + gi
    m_new = jnp.maximum(b_last + m_prev, jnp.max(g_end, axis=0, keepdims=True))
    decay = jnp.exp(b_last + m_prev - m_new)
    kw = k.astype(F32) * jnp.exp(g_end - m_new)
    c_new = decay * c_prev + lax.dot_general(
        kw.astype(BF16), v, (((0,), (0,)), ((), ())), preferred_element_type=F32)

    s_qk = qk * decay_mat
    num = w_inter * qc + jnp.dot(s_qk.astype(BF16), v, preferred_element_type=F32)
    qn = jnp.sum(q.astype(F32) * n_ref[...], axis=1, keepdims=True)
    den = w_inter * qn + jnp.sum(s_qk, axis=1, keepdims=True)
    hs = num / jnp.maximum(jnp.abs(den), jnp.exp(-m_t))

    ms = jnp.mean(hs * hs, axis=-1, keepdims=True)
    hs = hs * lax.rsqrt(ms + RMS_EPS) * hg_ref[...]
    o_ref[...] = (hs * _sigmoid(og_ref[...].astype(F32))).astype(o_ref.dtype)

    c_ref[...] = c_new
    n_ref[...] = decay * n_ref[...] + jnp.sum(kw, axis=0, keepdims=True)
    m_ref[...] = m_new


def _mlstm_cell(proj, gates, bias_row, head_gain, *, batch, seq, heads, dk, dv, chunk):
    t = proj.shape[0]
    nc = seq // chunk
    kq = heads
    vq = 2 * heads * dk // dv
    oq = vq + heads
    row = lambda b, h, c: b * nc + c
    return pl.pallas_call(
        functools.partial(_mlstm_kernel, dk=dk),
        grid=(batch, heads, nc),
        in_specs=[
            pl.BlockSpec((chunk, dk), lambda b, h, c: (row(b, h, c), h)),
            pl.BlockSpec((chunk, dk), lambda b, h, c: (row(b, h, c), kq + h)),
            pl.BlockSpec((chunk, dv), lambda b, h, c: (row(b, h, c), vq + h)),
            pl.BlockSpec((chunk, dv), lambda b, h, c: (row(b, h, c), oq + h)),
            pl.BlockSpec((chunk, LANES), lambda b, h, c: (row(b, h, c), 0)),
            pl.BlockSpec((1, LANES), lambda b, h, c: (0, 0)),
            pl.BlockSpec((1, dv), lambda b, h, c: (0, h)),
        ],
        out_specs=pl.BlockSpec((chunk, dv), lambda b, h, c: (row(b, h, c), h)),
        out_shape=jax.ShapeDtypeStruct((t, heads * dv), BF16),
        scratch_shapes=[pltpu.VMEM((dk, dv), F32), pltpu.VMEM((1, dk), F32), pltpu.VMEM((1, 1), F32)],
        compiler_params=_params("parallel", "parallel", "arbitrary"),
        name="mlstm_cell",
    )(proj, proj, proj, proj, gates, bias_row, head_gain.reshape(1, heads * dv))


def _sb_kernel(q_ref, k_ref, v_ref, o_ref, *, tile, sub):
    i = pl.program_id(2)
    d = q_ref.shape[1]
    row = lax.broadcasted_iota(jnp.int32, (tile, tile), 0)
    col = lax.broadcasted_iota(jnp.int32, (tile, tile), 1)
    later = (row > col).astype(BF16)
    later2 = jnp.concatenate([later, later], axis=0)
    strict = col < row
    qs = [q_ref[r * tile:(r + 1) * tile, :] for r in range(sub)]

    def sweep(kbs, cs, accs, diagonal):
        n_t = range(sub)
        starts = [pl.multiple_of(kb * tile, tile) for kb in kbs]
        zs = [lax.dot_general(qs[r], k_ref[pl.ds(starts[r], tile), :], (((1,), (1,)), ((), ())),
                              preferred_element_type=F32) for r in n_t]
        log_betas, log_keeps, splits = [], [], []
        for r in n_t:
            z = zs[r]
            sp = jnp.log(1.0 + jnp.exp2(jnp.abs(z) * -LOG2_E))
            log_beta = jnp.minimum(z, 0.0) - sp
            log_betas.append(log_beta)
            log_keep = log_beta - z
            if diagonal:
                log_keep = jnp.where(strict, log_keep, 0.0)
            log_keeps.append(log_keep)
            hi = log_keep.astype(BF16)
            lo = (log_keep - hi.astype(F32)).astype(BF16)
            splits.append(jnp.concatenate([hi, lo], axis=1))
        afters = [jnp.dot(splits[r], later2, preferred_element_type=F32) for r in n_t]
        weights = []
        for r in n_t:
            a = jnp.exp(log_betas[r] + afters[r] + cs[r])
            if diagonal:
                a = jnp.where(strict, a, 0.0)
            weights.append(a.astype(BF16))
        new_accs = [accs[r] + jnp.dot(weights[r], v_ref[pl.ds(starts[r], tile), :], preferred_element_type=F32)
                    for r in n_t]
        new_cs = [jnp.where(kbs[r] <= 0, -jnp.inf, cs[r] + afters[r][:, 0:1] + log_keeps[r][:, 0:1]) for r in n_t]
        return new_cs, new_accs

    def unfinished(cs):
        worst = functools.reduce(jnp.maximum, [jnp.max(c, axis=0, keepdims=True) for c in cs])
        return jnp.max(worst) >= F32_EXP_ZERO

    cs, accs = sweep([i * sub + r for r in range(sub)],
                     [jnp.zeros((tile, 1), F32)] * sub, [jnp.zeros((tile, d), F32)] * sub, True)

    def body(carry):
        n, _, cs, accs = carry
        kbs = [jnp.maximum(i * sub + r - n, 0) for r in range(sub)]
        cs, accs = sweep(kbs, cs, accs, False)
        return n + 1, unfinished(cs), tuple(cs), tuple(accs)

    init = (jnp.int32(1), unfinished(cs), tuple(cs), tuple(accs))
    _, _, _, accs = lax.while_loop(lambda carry: carry[1], body, init)
    for r in range(sub):
        o_ref[r * tile:(r + 1) * tile, :] = accs[r].astype(o_ref.dtype)


def _sb_attention(q, kv, *, batch, seq, heads, tile, sub):
    d = SB_HEAD_DIM
    tq = tile * sub
    return pl.pallas_call(
        functools.partial(_sb_kernel, tile=tile, sub=sub),
        grid=(batch, heads, seq // tq),
        in_specs=[
            pl.BlockSpec((None, tq, d), lambda b, h, i: (b, i, h)),
            pl.BlockSpec((None, seq, d), lambda b, h, i: (b, 0, h)),
            pl.BlockSpec((None, seq, d), lambda b, h, i: (b, 0, heads + h)),
        ],
        out_specs=pl.BlockSpec((None, tq, d), lambda b, h, i: (b, i, h)),
        out_shape=jax.ShapeDtypeStruct((batch, seq, heads * d), BF16),
        compiler_params=_params("parallel", "parallel", "arbitrary"),
        name="sb_attention",
    )(q, kv, kv)


def kernel(x, p, ffn_norm, ffn_w_gate, ffn_w_up, ffn_w_down, mix_norm, mlstm_w_in, mlstm_b_if,
           mlstm_head_norm, mlstm_w_out, kv_norm, sb_w_kv, sb_k_norm, sb_w_q, sb_q_norm, sb_w_o,
           ple_norm, ple_w_proj, ple_w_gate):
    batch, seq, d = x.shape
    depth = p.shape[0]
    n_a = mlstm_w_in.shape[0]
    t = batch * seq
    heads = FORGET_HEADS
    dv = mlstm_w_out.shape[1] // heads
    dk = (mlstm_w_in.shape[2] - 2 * heads * dv - 2 * heads) // (2 * heads)
    n_main = 2 * heads * dk + 2 * heads * dv
    sb_heads = sb_w_q.shape[2] // SB_HEAD_DIM
    tm, tmr = ROW_TILE, ROW_TILE_RESIDENT

    wg_all = ffn_w_gate.astype(BF16)
    wu_all = ffn_w_up.astype(BF16)
    wd_all = ffn_w_down.astype(BF16)
    w_in = mlstm_w_in.astype(BF16)
    w_in_gates = jnp.pad(mlstm_w_in[:, :, n_main:].astype(BF16), ((0, 0), (0, 0), (0, LANES - 2 * heads)))
    bias_rows = jnp.pad(mlstm_b_if.astype(F32), ((0, 0), (0, LANES - 2 * heads)))
    w_out = mlstm_w_out.astype(BF16)
    w_kv = sb_w_kv.astype(BF16)
    w_q = sb_w_q.astype(BF16)
    w_o = sb_w_o.astype(BF16)
    w_pp = ple_w_proj.astype(BF16)
    w_pg = ple_w_gate.astype(BF16)

    h = x.reshape(t, d)
    p2 = p.reshape(depth, t, p.shape[-1])
    kv_sh = None
    for i in range(depth):
        if i == n_a:
            kv_sh = _norm_proj(h, kv_norm, w_kv, sb_k_norm, normed_cols=sb_heads * SB_HEAD_DIM,
                               tm=tm, tn=2 * COL_CHUNK, name="kv_proj").reshape(batch, seq, -1)
        h = _ffn(h, ffn_norm[i, 0], wg_all, wu_all, wd_all, layer=i, half=0, tm=tm, tf=FFN_COLS)
        if i < n_a:
            proj, gates = _mlstm_in(h, mix_norm[i], w_in, w_in_gates[i], layer=i, n=n_main, tm=tm, tn=2 * COL_CHUNK)
            hs = _mlstm_cell(proj, gates, bias_rows[i:i + 1], mlstm_head_norm[i], batch=batch, seq=seq,
                             heads=heads, dk=dk, dv=dv, chunk=MLSTM_CHUNK)
            h = _proj_res(hs, w_out, h, layer=i, tm=tmr, name="mlstm_out")
        else:
            j = i - n_a
            q = _norm_proj_resident(h, mix_norm[i], w_q, sb_q_norm[j], layer=j, scale=SB_HEAD_DIM ** -0.5,
                                    tm=tm, name="q_proj")
            o = _sb_attention(q.reshape(batch, seq, -1), kv_sh, batch=batch, seq=seq, heads=sb_heads,
                              tile=SB_TILE, sub=SB_SUB)
            h = _proj_res(o.reshape(t, -1), w_o, h, layer=j, tm=tmr, name="sb_out")
        h = _ffn(h, ffn_norm[i, 1], wg_all, wu_all, wd_all, layer=i, half=1, tm=tm, tf=FFN_COLS)
        h = _ple(h, p2, ple_norm[i], w_pp, w_pg, layer=i, tm=tmr)
    return h.reshape(batch, seq, d)
```

```python
import functools

import jax
import jax.numpy as jnp
from jax import lax
from jax.experimental import pallas as pl
from jax.experimental.pallas import tpu as pltpu

F32 = jnp.float32
BF16 = jnp.bfloat16

RMS_EPS = 1e-6
FORGET_HEADS = 4
SB_HEAD_DIM = 128
LANES = 128
MXU_DIM = 256
VMEM_LIMIT = 56 * 1024 * 1024

ROW_TILE = 1024
ROW_TILE_RESIDENT = 512
COL_CHUNK = 512
FFN_COLS = 2 * MXU_DIM
NORM_ROWS = 256
MLSTM_CHUNK = 256
SB_TILE = 256
SB_SUB = 8
F32_EXP_ZERO = -104.0
LOG2_E = 1.4426950408889634


def _params(*sem):
    return pltpu.CompilerParams(dimension_semantics=sem, vmem_limit_bytes=VMEM_LIMIT)


def _sigmoid(x):
    return 1.0 / (1.0 + jnp.exp(-x))


def _softplus_neg_abs(x):
    return jnp.log1p(jnp.exp(-jnp.abs(x)))


def _rmsnorm_rows(x_ref, g_ref, xn_ref):
    rows = x_ref.shape[0]
    step = min(NORM_ROWS, rows)

    def body(r, carry):
        sl = pl.ds(pl.multiple_of(r * step, step), step)
        x = x_ref[sl, :]
        ms = jnp.mean(x * x, axis=-1, keepdims=True)
        xn_ref[sl, :] = (x * lax.rsqrt(ms + RMS_EPS) * g_ref[...]).astype(BF16)
        return carry

    lax.fori_loop(0, rows // step, body, 0)


def _col_chunks(n):
    step = min(COL_CHUNK, n)
    return [slice(c, c + step) for c in range(0, n, step)]


def _ffn_kernel(x_ref, g_ref, wg_ref, wu_ref, wd_ref, o_ref, xn_ref, *, last_cols):
    j = pl.program_id(1)
    last = pl.num_programs(1) - 1
    tf = wg_ref.shape[1]

    @pl.when(j == 0)
    def _():
        _rmsnorm_rows(x_ref, g_ref, xn_ref)
        o_ref[...] = x_ref[...]

    def accumulate(cols):
        xn = xn_ref[...]
        gate = jnp.dot(xn, wg_ref[:, :cols], preferred_element_type=F32)
        up = jnp.dot(xn, wu_ref[:, :cols], preferred_element_type=F32)
        hid = (gate * _sigmoid(gate) * up * 0.5).astype(BF16)
        for cs in _col_chunks(o_ref.shape[1]):
            o_ref[:, cs] += jnp.dot(hid, wd_ref[:cols, cs], preferred_element_type=F32)

    if last_cols == tf:
        accumulate(tf)
    else:
        pl.when(j < last)(lambda: accumulate(tf))
        pl.when(j == last)(lambda: accumulate(last_cols))


def _ffn(h, g, wg, wu, wd, *, layer, half, tm, tf):
    t, d = h.shape
    ff = wg.shape[-1]
    steps = pl.cdiv(ff, tf)
    return pl.pallas_call(
        functools.partial(_ffn_kernel, last_cols=ff - (steps - 1) * tf),
        grid=(t // tm, steps),
        in_specs=[
            pl.BlockSpec((tm, d), lambda i, j: (i, 0)),
            pl.BlockSpec((1, d), lambda i, j: (0, 0)),
            pl.BlockSpec((None, None, d, tf), lambda i, j: (layer, half, 0, j)),
            pl.BlockSpec((None, None, d, tf), lambda i, j: (layer, half, 0, j)),
            pl.BlockSpec((None, None, tf, d), lambda i, j: (layer, half, j, 0)),
        ],
        out_specs=pl.BlockSpec((tm, d), lambda i, j: (i, 0)),
        out_shape=jax.ShapeDtypeStruct((t, d), F32),
        scratch_shapes=[pltpu.VMEM((tm, d), BF16)],
        compiler_params=_params("parallel", "arbitrary"),
        name="ffn",
    )(h, g.reshape(1, d), wg, wu, wd)


def _head_norm(y, gain_row, scale):
    outs = []
    for s in range(y.shape[1] // SB_HEAD_DIM):
        ys = y[:, s * SB_HEAD_DIM:(s + 1) * SB_HEAD_DIM]
        ms = jnp.mean(ys * ys, axis=-1, keepdims=True)
        outs.append(ys * lax.rsqrt(ms + RMS_EPS) * gain_row * scale)
    return jnp.concatenate(outs, axis=1) if len(outs) > 1 else outs[0]


def _norm_proj_kernel(x_ref, g_ref, w_ref, hg_ref, o_ref, xn_ref, *, normed_blocks, scale):
    j = pl.program_id(1)

    @pl.when(j == 0)
    def _():
        _rmsnorm_rows(x_ref, g_ref, xn_ref)

    y = jnp.dot(xn_ref[...], w_ref[...], preferred_element_type=F32)

    @pl.when(j < normed_blocks)
    def _():
        o_ref[...] = _head_norm(y, hg_ref[...], scale).astype(o_ref.dtype)

    @pl.when(j >= normed_blocks)
    def _():
        o_ref[...] = y.astype(o_ref.dtype)


def _norm_proj(h, g, w, head_gain, *, normed_cols, tm, tn, name):
    t, d = h.shape
    n = w.shape[1]
    return pl.pallas_call(
        functools.partial(_norm_proj_kernel, normed_blocks=normed_cols // tn, scale=1.0),
        grid=(t // tm, n // tn),
        in_specs=[
            pl.BlockSpec((tm, d), lambda i, j: (i, 0)),
            pl.BlockSpec((1, d), lambda i, j: (0, 0)),
            pl.BlockSpec((d, tn), lambda i, j: (0, j)),
            pl.BlockSpec((1, SB_HEAD_DIM), lambda i, j: (0, 0)),
        ],
        out_specs=pl.BlockSpec((tm, tn), lambda i, j: (i, j)),
        out_shape=jax.ShapeDtypeStruct((t, n), BF16),
        scratch_shapes=[pltpu.VMEM((tm, d), BF16)],
        compiler_params=_params("parallel", "arbitrary"),
        name=name,
    )(h, g.reshape(1, d), w, head_gain.reshape(1, SB_HEAD_DIM).astype(F32))


def _norm_proj_resident_kernel(x_ref, g_ref, w_ref, hg_ref, o_ref, xn_ref, *, scale):
    _rmsnorm_rows(x_ref, g_ref, xn_ref)
    xn = xn_ref[...]
    for cs in _col_chunks(o_ref.shape[1]):
        y = jnp.dot(xn, w_ref[:, cs], preferred_element_type=F32)
        o_ref[:, cs] = _head_norm(y, hg_ref[...], scale).astype(o_ref.dtype)


def _norm_proj_resident(h, g, w, head_gain, *, layer, scale, tm, name):
    t, d = h.shape
    n = w.shape[2]
    return pl.pallas_call(
        functools.partial(_norm_proj_resident_kernel, scale=scale),
        grid=(t // tm,),
        in_specs=[
            pl.BlockSpec((tm, d), lambda i: (i, 0)),
            pl.BlockSpec((1, d), lambda i: (0, 0)),
            pl.BlockSpec((None, d, n), lambda i: (layer, 0, 0)),
            pl.BlockSpec((1, SB_HEAD_DIM), lambda i: (0, 0)),
        ],
        out_specs=pl.BlockSpec((tm, n), lambda i: (i, 0)),
        out_shape=jax.ShapeDtypeStruct((t, n), BF16),
        scratch_shapes=[pltpu.VMEM((tm, d), BF16)],
        compiler_params=_params("parallel"),
        name=name,
    )(h, g.reshape(1, d), w, head_gain.reshape(1, SB_HEAD_DIM).astype(F32))


def _mlstm_in_kernel(x_ref, g_ref, w_ref, wgate_ref, o_ref, gates_ref, xn_ref):
    j = pl.program_id(1)

    @pl.when(j == 0)
    def _():
        _rmsnorm_rows(x_ref, g_ref, xn_ref)
        gates_ref[...] = jnp.dot(xn_ref[...], wgate_ref[...], preferred_element_type=F32)

    o_ref[...] = jnp.dot(xn_ref[...], w_ref[...], preferred_element_type=F32).astype(o_ref.dtype)


def _mlstm_in(h, g, w, w_gates, *, layer, n, tm, tn):
    t, d = h.shape
    return pl.pallas_call(
        _mlstm_in_kernel,
        grid=(t // tm, n // tn),
        in_specs=[
            pl.BlockSpec((tm, d), lambda i, j: (i, 0)),
            pl.BlockSpec((1, d), lambda i, j: (0, 0)),
            pl.BlockSpec((None, d, tn), lambda i, j: (layer, 0, j)),
            pl.BlockSpec((d, LANES), lambda i, j: (0, 0)),
        ],
        out_specs=[pl.BlockSpec((tm, tn), lambda i, j: (i, j)),
                   pl.BlockSpec((tm, LANES), lambda i, j: (i, 0))],
        out_shape=[jax.ShapeDtypeStruct((t, n), BF16), jax.ShapeDtypeStruct((t, LANES), F32)],
        scratch_shapes=[pltpu.VMEM((tm, d), BF16)],
        compiler_params=_params("parallel", "arbitrary"),
        name="mlstm_in",
    )(h, g.reshape(1, d), w, w_gates)


def _proj_res_kernel(a_ref, w_ref, r_ref, o_ref):
    a = a_ref[...]
    for cs in _col_chunks(o_ref.shape[1]):
        o_ref[:, cs] = r_ref[:, cs] + jnp.dot(a, w_ref[:, cs], preferred_element_type=F32)


def _proj_res(a, w, res, *, layer, tm, name):
    t, k = a.shape
    n = w.shape[2]
    return pl.pallas_call(
        _proj_res_kernel,
        grid=(t // tm,),
        in_specs=[
            pl.BlockSpec((tm, k), lambda i: (i, 0)),
            pl.BlockSpec((None, k, n), lambda i: (layer, 0, 0)),
            pl.BlockSpec((tm, n), lambda i: (i, 0)),
        ],
        out_specs=pl.BlockSpec((tm, n), lambda i: (i, 0)),
        out_shape=jax.ShapeDtypeStruct((t, n), F32),
        compiler_params=_params("parallel"),
        name=name,
    )(a, w, res)


def _ple_kernel(x_ref, g_ref, wg_ref, p_ref, wp_ref, o_ref, xn_ref):
    _rmsnorm_rows(x_ref, g_ref, xn_ref)
    xn = xn_ref[...]
    pb = p_ref[...].astype(BF16)
    for cs in _col_chunks(o_ref.shape[1]):
        gate = _sigmoid(jnp.dot(xn, wg_ref[:, cs], preferred_element_type=F32))
        emb = jnp.dot(pb, wp_ref[:, cs], preferred_element_type=F32)
        o_ref[:, cs] = x_ref[:, cs] + emb * gate


def _ple(h, p, g, w_proj, w_gate, *, layer, tm):
    t, d = h.shape
    dp = p.shape[2]
    return pl.pallas_call(
        _ple_kernel,
        grid=(t // tm,),
        in_specs=[
            pl.BlockSpec((tm, d), lambda i: (i, 0)),
            pl.BlockSpec((1, d), lambda i: (0, 0)),
            pl.BlockSpec((None, d, d), lambda i: (layer, 0, 0)),
            pl.BlockSpec((None, tm, dp), lambda i: (layer, i, 0)),
            pl.BlockSpec((None, dp, d), lambda i: (layer, 0, 0)),
        ],
        out_specs=pl.BlockSpec((tm, d), lambda i: (i, 0)),
        out_shape=jax.ShapeDtypeStruct((t, d), F32),
        scratch_shapes=[pltpu.VMEM((tm, d), BF16)],
        compiler_params=_params("parallel"),
        name="ple",
    )(h, g.reshape(1, d), w_gate, p, w_proj)


def _mlstm_kernel(q_ref, k_ref, v_ref, og_ref, gates_ref, bias_ref, hg_ref, o_ref,
                  c_ref, n_ref, m_ref, *, heads, dk, dv):
    chunk = pl.program_id(1)
    L = q_ref.shape[0]
    hs_range = range(heads)

    @pl.when(chunk == 0)
    def _():
        c_ref[...] = jnp.zeros_like(c_ref)
        n_ref[...] = jnp.zeros_like(n_ref)
        m_ref[...] = jnp.zeros_like(m_ref)

    qs = [q_ref[:, h * dk:(h + 1) * dk] * (dk ** -0.5) for h in hs_range]
    ks = [k_ref[:, h * dk:(h + 1) * dk] for h in hs_range]
    vs = [v_ref[:, h * dv:(h + 1) * dv] for h in hs_range]
    c_prevs = [c_ref[h] for h in hs_range]
    m_prevs = [m_ref[h] for h in hs_range]
    qks = [lax.dot_general(qs[h], ks[h], (((1,), (1,)), ((), ())), preferred_element_type=F32) for h in hs_range]
    qcs = [jnp.dot(qs[h], c_prevs[h].astype(BF16), preferred_element_type=F32) for h in hs_range]

    gates = gates_ref[...] + bias_ref[...]
    lane = lax.broadcasted_iota(jnp.int32, gates.shape, 1)
    row = lax.broadcasted_iota(jnp.int32, (L, L), 0)
    col = lax.broadcasted_iota(jnp.int32, (L, L), 1)
    eye = row == col
    causal = col <= row

    gis, b_cols, b_rows, gi_rows = [], [], [], []
    for h in hs_range:
        gi = jnp.sum(jnp.where(lane == h, gates, 0.0), axis=1, keepdims=True)
        gf = jnp.sum(jnp.where(lane == h + heads, gates, 0.0), axis=1, keepdims=True)
        lf = jnp.minimum(gf, 0.0) - _softplus_neg_abs(gf)
        lf_b = jnp.broadcast_to(lf, (L, L))
        lf_row = jnp.sum(jnp.where(eye, lf_b, 0.0), axis=0, keepdims=True)
        gis.append(gi)
        gi_rows.append(jnp.sum(jnp.where(eye, jnp.broadcast_to(gi, (L, L)), 0.0), axis=0, keepdims=True))
        b_rows.append(jnp.sum(jnp.where(row <= col, lf_b, 0.0), axis=0, keepdims=True))
        b_cols.append(jnp.sum(jnp.where(causal, jnp.broadcast_to(lf_row, (L, L)), 0.0), axis=1, keepdims=True))

    decays, kws, m_news, c_news = [], [], [], []
    for h in hs_range:
        b_last = b_cols[h][L - 1:L, :]
        g_end = b_last - b_cols[h] + gis[h]
        m_new = jnp.maximum(b_last + m_prevs[h], jnp.max(g_end, axis=0, keepdims=True))
        decays.append(jnp.exp(b_last + m_prevs[h] - m_new))
        kws.append(ks[h].astype(F32) * jnp.exp(g_end - m_new))
        m_news.append(m_new)
    for h in hs_range:
        c_news.append(decays[h] * c_prevs[h] + lax.dot_general(
            kws[h].astype(BF16), vs[h], (((0,), (0,)), ((), ())), preferred_element_type=F32))

    m_ts, w_inters, s_qks = [], [], []
    for h in hs_range:
        a_col = b_cols[h] + m_prevs[h]
        dm = jnp.where(causal, b_cols[h] - b_rows[h] + gi_rows[h], -jnp.inf)
        m_t = jnp.maximum(a_col, jnp.max(dm, axis=1, keepdims=True))
        m_ts.append(m_t)
        w_inters.append(jnp.exp(a_col - m_t))
        s_qks.append(qks[h] * jnp.exp(dm - m_t))
    svs = [jnp.dot(s_qks[h].astype(BF16), vs[h], preferred_element_type=F32) for h in hs_range]

    for h in hs_range:
        num = w_inters[h] * qcs[h] + svs[h]
        qn = jnp.sum(qs[h].astype(F32) * n_ref[h], axis=1, keepdims=True)
        den = w_inters[h] * qn + jnp.sum(s_qks[h], axis=1, keepdims=True)
        hs = num / jnp.maximum(jnp.abs(den), jnp.exp(-m_ts[h]))
        ms = jnp.mean(hs * hs, axis=-1, keepdims=True)
        cols = slice(h * dv, (h + 1) * dv)
        hs = hs * lax.rsqrt(ms + RMS_EPS) * hg_ref[:, cols]
        o_ref[:, cols] = (hs * _sigmoid(og_ref[:, cols].astype(F32))).astype(o_ref.dtype)

        c_ref[h] = c_news[h]
        n_ref[h] = decays[h] * n_ref[h] + jnp.sum(kws[h], axis=0, keepdims=True)
        m_ref[h] = m_news[h]


def _mlstm_cell(proj, gates, bias_row, head_gain, *, batch, seq, heads, dk, dv, chunk):
    t = proj.shape[0]
    nc = seq // chunk
    wq, wv = heads * dk, heads * dv
    assert 2 * wq == wv, "the column-block indices below assume the q|k and v|o-gate widths of the xLSTM ratios"
    row = lambda b, c: b * nc + c
    return pl.pallas_call(
        functools.partial(_mlstm_kernel, heads=heads, dk=dk, dv=dv),
        grid=(batch, nc),
        in_specs=[
            pl.BlockSpec((chunk, wq), lambda b, c: (row(b, c), 0)),
            pl.BlockSpec((chunk, wq), lambda b, c: (row(b, c), 1)),
            pl.BlockSpec((chunk, wv), lambda b, c: (row(b, c), 1)),
            pl.BlockSpec((chunk, wv), lambda b, c: (row(b, c), 2)),
            pl.BlockSpec((chunk, LANES), lambda b, c: (row(b, c), 0)),
            pl.BlockSpec((1, LANES), lambda b, c: (0, 0)),
            pl.BlockSpec((1, wv), lambda b, c: (0, 0)),
        ],
        out_specs=pl.BlockSpec((chunk, wv), lambda b, c: (row(b, c), 0)),
        out_shape=jax.ShapeDtypeStruct((t, wv), BF16),
        scratch_shapes=[pltpu.VMEM((heads, dk, dv), F32), pltpu.VMEM((heads, 1, dk), F32),
                        pltpu.VMEM((heads, 1, 1), F32)],
        compiler_params=_params("parallel", "arbitrary"),
        name="mlstm_cell",
    )(proj, proj, proj, proj, gates, bias_row, head_gain.reshape(1, wv))


def _sb_kernel(q_ref, k_ref, v_ref, o_ref, *, tile, sub):
    i = pl.program_id(2)
    d = q_ref.shape[1]
    row = lax.broadcasted_iota(jnp.int32, (tile, tile), 0)
    col = lax.broadcasted_iota(jnp.int32, (tile, tile), 1)
    later = (row > col).astype(BF16)
    later2 = jnp.concatenate([later, later], axis=0)
    strict = col < row
    qs = [q_ref[r * tile:(r + 1) * tile, :] for r in range(sub)]

    def sweep(kbs, cs, accs, diagonal):
        n_t = range(sub)
        starts = [pl.multiple_of(kb * tile, tile) for kb in kbs]
        zs = [lax.dot_general(qs[r], k_ref[pl.ds(starts[r], tile), :], (((1,), (1,)), ((), ())),
                              preferred_element_type=F32) for r in n_t]
        log_betas, log_keeps, splits = [], [], []
        for r in n_t:
            z = zs[r]
            sp = jnp.log(1.0 + jnp.exp2(jnp.abs(z) * -LOG2_E))
            log_beta = jnp.minimum(z, 0.0) - sp
            log_betas.append(log_beta)
            log_keep = log_beta - z
            if diagonal:
                log_keep = jnp.where(strict, log_keep, 0.0)
            log_keeps.append(log_keep)
            hi = log_keep.astype(BF16)
            lo = (log_keep - hi.astype(F32)).astype(BF16)
            splits.append(jnp.concatenate([hi, lo], axis=1))
        afters = [jnp.dot(splits[r], later2, preferred_element_type=F32) for r in n_t]
        weights = []
        for r in n_t:
            a = jnp.exp(log_betas[r] + afters[r] + cs[r])
            if diagonal:
                a = jnp.where(strict, a, 0.0)
            weights.append(a.astype(BF16))
        new_accs = [accs[r] + jnp.dot(weights[r], v_ref[pl.ds(starts[r], tile), :], preferred_element_type=F32)
                    for r in n_t]
        new_cs = [jnp.where(kbs[r] <= 0, -jnp.inf, cs[r] + afters[r][:, 0:1] + log_keeps[r][:, 0:1]) for r in n_t]
        return new_cs, new_accs

    def unfinished(cs):
        worst = functools.reduce(jnp.maximum, [jnp.max(c, axis=0, keepdims=True) for c in cs])
        return jnp.max(worst) >= F32_EXP_ZERO

    cs, accs = sweep([i * sub + r for r in range(sub)],
                     [jnp.zeros((tile, 1), F32)] * sub, [jnp.zeros((tile, d), F32)] * sub, True)

    def body(carry):
        n, _, cs, accs = carry
        kbs = [jnp.maximum(i * sub + r - n, 0) for r in range(sub)]
        cs, accs = sweep(kbs, cs, accs, False)
        return n + 1, unfinished(cs), tuple(cs), tuple(accs)

    init = (jnp.int32(1), unfinished(cs), tuple(cs), tuple(accs))
    _, _, _, accs = lax.while_loop(lambda carry: carry[1], body, init)
    for r in range(sub):
        o_ref[r * tile:(r + 1) * tile, :] = accs[r].astype(o_ref.dtype)


def _sb_attention(q, kv, *, batch, seq, heads, tile, sub):
    d = SB_HEAD_DIM
    tq = tile * sub
    return pl.pallas_call(
        functools.partial(_sb_kernel, tile=tile, sub=sub),
        grid=(batch, heads, seq // tq),
        in_specs=[
            pl.BlockSpec((None, tq, d), lambda b, h, i: (b, i, h)),
            pl.BlockSpec((None, seq, d), lambda b, h, i: (b, 0, h)),
            pl.BlockSpec((None, seq, d), lambda b, h, i: (b, 0, heads + h)),
        ],
        out_specs=pl.BlockSpec((None, tq, d), lambda b, h, i: (b, i, h)),
        out_shape=jax.ShapeDtypeStruct((batch, seq, heads * d), BF16),
        compiler_params=_params("parallel", "parallel", "arbitrary"),
        name="sb_attention",
    )(q, kv, kv)


def kernel(x, p, ffn_norm, ffn_w_gate, ffn_w_up, ffn_w_down, mix_norm, mlstm_w_in, mlstm_b_if,
           mlstm_head_norm, mlstm_w_out, kv_norm, sb_w_kv, sb_k_norm, sb_w_q, sb_q_norm, sb_w_o,
           ple_norm, ple_w_proj, ple_w_gate):
    batch, seq, d = x.shape
    depth = p.shape[0]
    n_a = mlstm_w_in.shape[0]
    t = batch * seq
    heads = FORGET_HEADS
    dv = mlstm_w_out.shape[1] // heads
    dk = (mlstm_w_in.shape[2] - 2 * heads * dv - 2 * heads) // (2 * heads)
    n_main = 2 * heads * dk + 2 * heads * dv
    sb_heads = sb_w_q.shape[2] // SB_HEAD_DIM
    tm, tmr = ROW_TILE, ROW_TILE_RESIDENT

    wg_all = ffn_w_gate.astype(BF16)
    wu_all = ffn_w_up.astype(BF16)
    wd_all = ffn_w_down.astype(BF16)
    w_in = mlstm_w_in.astype(BF16)
    w_in_gates = jnp.pad(mlstm_w_in[:, :, n_main:].astype(BF16), ((0, 0), (0, 0), (0, LANES - 2 * heads)))
    bias_rows = jnp.pad(mlstm_b_if.astype(F32), ((0, 0), (0, LANES - 2 * heads)))
    w_out = mlstm_w_out.astype(BF16)
    w_kv = sb_w_kv.astype(BF16)
    w_q = sb_w_q.astype(BF16)
    w_o = sb_w_o.astype(BF16)
    w_pp = ple_w_proj.astype(BF16)
    w_pg = ple_w_gate.astype(BF16)

    h = x.reshape(t, d)
    p2 = p.reshape(depth, t, p.shape[-1])
    kv_sh = None
    for i in range(depth):
        if i == n_a:
            kv_sh = _norm_proj(h, kv_norm, w_kv, sb_k_norm, normed_cols=sb_heads * SB_HEAD_DIM,
                               tm=tm, tn=2 * COL_CHUNK, name="kv_proj").reshape(batch, seq, -1)
        h = _ffn(h, ffn_norm[i, 0], wg_all, wu_all, wd_all, layer=i, half=0, tm=tm, tf=FFN_COLS)
        if i < n_a:
            proj, gates = _mlstm_in(h, mix_norm[i], w_in, w_in_gates[i], layer=i, n=n_main, tm=tm, tn=2 * COL_CHUNK)
            hs = _mlstm_cell(proj, gates, bias_rows[i:i + 1], mlstm_head_norm[i], batch=batch, seq=seq,
                             heads=heads, dk=dk, dv=dv, chunk=MLSTM_CHUNK)
            h = _proj_res(hs, w_out, h, layer=i, tm=tmr, name="mlstm_out")
        else:
            j = i - n_a
            q = _norm_proj_resident(h, mix_norm[i], w_q, sb_q_norm[j], layer=j, scale=SB_HEAD_DIM ** -0.5,
                                    tm=tm, name="q_proj")
            o = _sb_attention(q.reshape(batch, seq, -1), kv_sh, batch=batch, seq=seq, heads=sb_heads,
                              tile=SB_TILE, sub=SB_SUB)
            h = _proj_res(o.reshape(t, -1), w_o, h, layer=j, tm=tmr, name="sb_out")
        h = _ffn(h, ffn_norm[i, 1], wg_all, wu_all, wd_all, layer=i, half=1, tm=tm, tf=FFN_COLS)
        h = _ple(h, p2, ple_norm[i], w_pp, w_pg, layer=i, tm=tmr)
    return h.reshape(batch, seq, d)
```

```python
import functools

import jax
import jax.numpy as jnp
from jax import lax
from jax.experimental import pallas as pl
from jax.experimental.pallas import tpu as pltpu

F32 = jnp.float32
BF16 = jnp.bfloat16

RMS_EPS = 1e-6
FORGET_HEADS = 4
SB_HEAD_DIM = 128
LANES = 128
MXU_DIM = 256
VMEM_LIMIT = 56 * 1024 * 1024

ROW_TILE = 1024
ROW_TILE_RESIDENT = 512
COL_CHUNK = 512
FFN_COLS = 2 * MXU_DIM
NORM_ROWS = 256
MLSTM_CHUNK = 256
SB_TILE = 256
SB_SUB = 8
F32_EXP_ZERO = -104.0
LOG2_E = 1.4426950408889634


def _params(*sem):
    return pltpu.CompilerParams(dimension_semantics=sem, vmem_limit_bytes=VMEM_LIMIT)


def _sigmoid(x):
    return 1.0 / (1.0 + jnp.exp(-x))


def _softplus_neg_abs(x):
    return jnp.log1p(jnp.exp(-jnp.abs(x)))


def _rmsnorm_rows(x_ref, g_ref, xn_ref):
    rows = x_ref.shape[0]
    step = min(NORM_ROWS, rows)

    def body(r, carry):
        sl = pl.ds(pl.multiple_of(r * step, step), step)
        x = x_ref[sl, :]
        ms = jnp.mean(x * x, axis=-1, keepdims=True)
        xn_ref[sl, :] = (x * lax.rsqrt(ms + RMS_EPS) * g_ref[...]).astype(BF16)
        return carry

    lax.fori_loop(0, rows // step, body, 0)


def _col_chunks(n):
    step = min(COL_CHUNK, n)
    return [slice(c, c + step) for c in range(0, n, step)]


def _ffn_kernel(x_ref, g_ref, wg_ref, wu_ref, wd_ref, o_ref, xn_ref, *, last_cols):
    j = pl.program_id(1)
    last = pl.num_programs(1) - 1
    tf = wg_ref.shape[1]

    def accumulate(cols, acc_ref):
        xn = xn_ref[...]
        gate = jnp.dot(xn, wg_ref[:, :cols], preferred_element_type=F32)
        up = jnp.dot(xn, wu_ref[:, :cols], preferred_element_type=F32)
        hid = (gate * _sigmoid(gate) * up * 0.5).astype(BF16)
        for cs in _col_chunks(o_ref.shape[1]):
            o_ref[:, cs] = acc_ref[:, cs] + jnp.dot(hid, wd_ref[:cols, cs], preferred_element_type=F32)

    @pl.when(j == 0)
    def _():
        _rmsnorm_rows(x_ref, g_ref, xn_ref)
        accumulate(tf, x_ref)

    if last_cols == tf:
        pl.when(j > 0)(lambda: accumulate(tf, o_ref))
    else:
        pl.when((j > 0) & (j < last))(lambda: accumulate(tf, o_ref))
        pl.when(j == last)(lambda: accumulate(last_cols, o_ref))


def _ffn(h, g, wg, wu, wd, *, layer, half, tm, tf):
    t, d = h.shape
    ff = wg.shape[-1]
    steps = pl.cdiv(ff, tf)
    assert ff >= tf, "the first hidden block is taken at full width"
    return pl.pallas_call(
        functools.partial(_ffn_kernel, last_cols=ff - (steps - 1) * tf),
        grid=(t // tm, steps),
        in_specs=[
            pl.BlockSpec((tm, d), lambda i, j: (i, 0)),
            pl.BlockSpec((1, d), lambda i, j: (0, 0)),
            pl.BlockSpec((None, None, d, tf), lambda i, j: (layer, half, 0, j)),
            pl.BlockSpec((None, None, d, tf), lambda i, j: (layer, half, 0, j)),
            pl.BlockSpec((None, None, tf, d), lambda i, j: (layer, half, j, 0)),
        ],
        out_specs=pl.BlockSpec((tm, d), lambda i, j: (i, 0)),
        out_shape=jax.ShapeDtypeStruct((t, d), F32),
        scratch_shapes=[pltpu.VMEM((tm, d), BF16)],
        compiler_params=_params("parallel", "arbitrary"),
        name="ffn",
    )(h, g.reshape(1, d), wg, wu, wd)


def _head_norm(y, gain_row, scale):
    outs = []
    for s in range(y.shape[1] // SB_HEAD_DIM):
        ys = y[:, s * SB_HEAD_DIM:(s + 1) * SB_HEAD_DIM]
        ms = jnp.mean(ys * ys, axis=-1, keepdims=True)
        outs.append(ys * lax.rsqrt(ms + RMS_EPS) * gain_row * scale)
    return jnp.concatenate(outs, axis=1) if len(outs) > 1 else outs[0]


def _norm_proj_kernel(x_ref, g_ref, w_ref, hg_ref, o_ref, xn_ref, *, normed_blocks, scale):
    j = pl.program_id(1)

    @pl.when(j == 0)
    def _():
        _rmsnorm_rows(x_ref, g_ref, xn_ref)

    def project(normed):
        xn = xn_ref[...]
        for cs in _col_chunks(o_ref.shape[1]):
            y = jnp.dot(xn, w_ref[:, cs], preferred_element_type=F32)
            if normed:
                y = _head_norm(y, hg_ref[...], scale)
            o_ref[:, cs] = y.astype(o_ref.dtype)

    pl.when(j < normed_blocks)(lambda: project(True))
    pl.when(j >= normed_blocks)(lambda: project(False))


def _norm_proj(h, g, w, head_gain, *, normed_cols, tm, tn, name):
    t, d = h.shape
    n = w.shape[1]
    return pl.pallas_call(
        functools.partial(_norm_proj_kernel, normed_blocks=normed_cols // tn, scale=1.0),
        grid=(t // tm, n // tn),
        in_specs=[
            pl.BlockSpec((tm, d), lambda i, j: (i, 0)),
            pl.BlockSpec((1, d), lambda i, j: (0, 0)),
            pl.BlockSpec((d, tn), lambda i, j: (0, j)),
            pl.BlockSpec((1, SB_HEAD_DIM), lambda i, j: (0, 0)),
        ],
        out_specs=pl.BlockSpec((tm, tn), lambda i, j: (i, j)),
        out_shape=jax.ShapeDtypeStruct((t, n), BF16),
        scratch_shapes=[pltpu.VMEM((tm, d), BF16)],
        compiler_params=_params("parallel", "arbitrary"),
        name=name,
    )(h, g.reshape(1, d), w, head_gain.reshape(1, SB_HEAD_DIM).astype(F32))


def _norm_proj_resident_kernel(x_ref, g_ref, w_ref, hg_ref, o_ref, xn_ref, *, scale):
    _rmsnorm_rows(x_ref, g_ref, xn_ref)
    xn = xn_ref[...]
    for cs in _col_chunks(o_ref.shape[1]):
        y = jnp.dot(xn, w_ref[:, cs], preferred_element_type=F32)
        o_ref[:, cs] = _head_norm(y, hg_ref[...], scale).astype(o_ref.dtype)


def _norm_proj_resident(h, g, w, head_gain, *, layer, scale, tm, name):
    t, d = h.shape
    n = w.shape[2]
    return pl.pallas_call(
        functools.partial(_norm_proj_resident_kernel, scale=scale),
        grid=(t // tm,),
        in_specs=[
            pl.BlockSpec((tm, d), lambda i: (i, 0)),
            pl.BlockSpec((1, d), lambda i: (0, 0)),
            pl.BlockSpec((None, d, n), lambda i: (layer, 0, 0)),
            pl.BlockSpec((1, SB_HEAD_DIM), lambda i: (0, 0)),
        ],
        out_specs=pl.BlockSpec((tm, n), lambda i: (i, 0)),
        out_shape=jax.ShapeDtypeStruct((t, n), BF16),
        scratch_shapes=[pltpu.VMEM((tm, d), BF16)],
        compiler_params=_params("parallel"),
        name=name,
    )(h, g.reshape(1, d), w, head_gain.reshape(1, SB_HEAD_DIM).astype(F32))


def _mlstm_in_kernel(x_ref, g_ref, w_ref, wgate_ref, o_ref, gates_ref, xn_ref):
    j = pl.program_id(1)

    @pl.when(j == 0)
    def _():
        _rmsnorm_rows(x_ref, g_ref, xn_ref)
        gates_ref[...] = jnp.dot(xn_ref[...], wgate_ref[...], preferred_element_type=F32)

    o_ref[...] = jnp.dot(xn_ref[...], w_ref[...], preferred_element_type=F32).astype(o_ref.dtype)


def _mlstm_in(h, g, w, w_gates, *, layer, n, tm, tn):
    t, d = h.shape
    return pl.pallas_call(
        _mlstm_in_kernel,
        grid=(t // tm, n // tn),
        in_specs=[
            pl.BlockSpec((tm, d), lambda i, j: (i, 0)),
            pl.BlockSpec((1, d), lambda i, j: (0, 0)),
            pl.BlockSpec((None, d, tn), lambda i, j: (layer, 0, j)),
            pl.BlockSpec((d, LANES), lambda i, j: (0, 0)),
        ],
        out_specs=[pl.BlockSpec((tm, tn), lambda i, j: (i, j)),
                   pl.BlockSpec((tm, LANES), lambda i, j: (i, 0))],
        out_shape=[jax.ShapeDtypeStruct((t, n), BF16), jax.ShapeDtypeStruct((t, LANES), F32)],
        scratch_shapes=[pltpu.VMEM((tm, d), BF16)],
        compiler_params=_params("parallel", "arbitrary"),
        name="mlstm_in",
    )(h, g.reshape(1, d), w, w_gates)


def _proj_res_kernel(a_ref, w_ref, r_ref, o_ref):
    a = a_ref[...]
    for cs in _col_chunks(o_ref.shape[1]):
        o_ref[:, cs] = r_ref[:, cs] + jnp.dot(a, w_ref[:, cs], preferred_element_type=F32)


def _proj_res(a, w, res, *, layer, tm, name):
    t, k = a.shape
    n = w.shape[2]
    return pl.pallas_call(
        _proj_res_kernel,
        grid=(t // tm,),
        in_specs=[
            pl.BlockSpec((tm, k), lambda i: (i, 0)),
            pl.BlockSpec((None, k, n), lambda i: (layer, 0, 0)),
            pl.BlockSpec((tm, n), lambda i: (i, 0)),
        ],
        out_specs=pl.BlockSpec((tm, n), lambda i: (i, 0)),
        out_shape=jax.ShapeDtypeStruct((t, n), F32),
        compiler_params=_params("parallel"),
        name=name,
    )(a, w, res)


def _ple_kernel(x_ref, g_ref, wg_ref, p_ref, wp_ref, o_ref, xn_ref):
    _rmsnorm_rows(x_ref, g_ref, xn_ref)
    xn = xn_ref[...]
    pb = p_ref[...].astype(BF16)
    for cs in _col_chunks(o_ref.shape[1]):
        gate = _sigmoid(jnp.dot(xn, wg_ref[:, cs], preferred_element_type=F32))
        emb = jnp.dot(pb, wp_ref[:, cs], preferred_element_type=F32)
        o_ref[:, cs] = x_ref[:, cs] + emb * gate


def _ple(h, p, g, w_proj, w_gate, *, layer, tm):
    t, d = h.shape
    dp = p.shape[2]
    return pl.pallas_call(
        _ple_kernel,
        grid=(t // tm,),
        in_specs=[
            pl.BlockSpec((tm, d), lambda i: (i, 0)),
            pl.BlockSpec((1, d), lambda i: (0, 0)),
            pl.BlockSpec((None, d, d), lambda i: (layer, 0, 0)),
            pl.BlockSpec((None, tm, dp), lambda i: (layer, i, 0)),
            pl.BlockSpec((None, dp, d), lambda i: (layer, 0, 0)),
        ],
        out_specs=pl.BlockSpec((tm, d), lambda i: (i, 0)),
        out_shape=jax.ShapeDtypeStruct((t, d), F32),
        scratch_shapes=[pltpu.VMEM((tm, d), BF16)],
        compiler_params=_params("parallel"),
        name="ple",
    )(h, g.reshape(1, d), w_gate, p, w_proj)


def _mlstm_kernel(q_ref, k_ref, v_ref, og_ref, gates_ref, bias_ref, hg_ref, o_ref,
                  c_ref, n_ref, m_ref, *, heads, dk, dv):
    chunk = pl.program_id(1)
    L = q_ref.shape[0]
    hs_range = range(heads)

    @pl.when(chunk == 0)
    def _():
        c_ref[...] = jnp.zeros_like(c_ref)
        n_ref[...] = jnp.zeros_like(n_ref)
        m_ref[...] = jnp.zeros_like(m_ref)

    qs = [q_ref[:, h * dk:(h + 1) * dk] * (dk ** -0.5) for h in hs_range]
    ks = [k_ref[:, h * dk:(h + 1) * dk] for h in hs_range]
    vs = [v_ref[:, h * dv:(h + 1) * dv] for h in hs_range]
    c_prevs = [c_ref[h] for h in hs_range]
    m_prevs = [m_ref[h] for h in hs_range]
    qks = [lax.dot_general(qs[h], ks[h], (((1,), (1,)), ((), ())), preferred_element_type=F32) for h in hs_range]
    qcs = [jnp.dot(qs[h], c_prevs[h].astype(BF16), preferred_element_type=F32) for h in hs_range]

    gates = gates_ref[...] + bias_ref[...]
    lane = lax.broadcasted_iota(jnp.int32, gates.shape, 1)
    row = lax.broadcasted_iota(jnp.int32, (L, L), 0)
    col = lax.broadcasted_iota(jnp.int32, (L, L), 1)
    eye = row == col
    causal = col <= row

    gis, b_cols, b_rows, gi_rows = [], [], [], []
    for h in hs_range:
        gi = jnp.sum(jnp.where(lane == h, gates, 0.0), axis=1, keepdims=True)
        gf = jnp.sum(jnp.where(lane == h + heads, gates, 0.0), axis=1, keepdims=True)
        lf = jnp.minimum(gf, 0.0) - _softplus_neg_abs(gf)
        lf_b = jnp.broadcast_to(lf, (L, L))
        lf_row = jnp.sum(jnp.where(eye, lf_b, 0.0), axis=0, keepdims=True)
        gis.append(gi)
        gi_rows.append(jnp.sum(jnp.where(eye, jnp.broadcast_to(gi, (L, L)), 0.0), axis=0, keepdims=True))
        b_rows.append(jnp.sum(jnp.where(row <= col, lf_b, 0.0), axis=0, keepdims=True))
        b_cols.append(jnp.sum(jnp.where(causal, jnp.broadcast_to(lf_row, (L, L)), 0.0), axis=1, keepdims=True))

    decays, kws, m_news, c_news = [], [], [], []
    for h in hs_range:
        b_last = b_cols[h][L - 1:L, :]
        g_end = b_last - b_cols[h] + gis[h]
        m_new = jnp.maximum(b_last + m_prevs[h], jnp.max(g_end, axis=0, keepdims=True))
        decays.append(jnp.exp(b_last + m_prevs[h] - m_new))
        kws.append(ks[h].astype(F32) * jnp.exp(g_end - m_new))
        m_news.append(m_new)
    for h in hs_range:
        c_news.append(decays[h] * c_prevs[h] + lax.dot_general(
            kws[h].astype(BF16), vs[h], (((0,), (0,)), ((), ())), preferred_element_type=F32))

    m_ts, w_inters, s_qks = [], [], []
    for h in hs_range:
        a_col = b_cols[h] + m_prevs[h]
        dm = jnp.where(causal, b_cols[h] - b_rows[h] + gi_rows[h], -jnp.inf)
        m_t = jnp.maximum(a_col, jnp.max(dm, axis=1, keepdims=True))
        m_ts.append(m_t)
        w_inters.append(jnp.exp(a_col - m_t))
        s_qks.append(qks[h] * jnp.exp(dm - m_t))
    svs = [jnp.dot(s_qks[h].astype(BF16), vs[h], preferred_element_type=F32) for h in hs_range]

    for h in hs_range:
        num = w_inters[h] * qcs[h] + svs[h]
        qn = jnp.sum(qs[h].astype(F32) * n_ref[h], axis=1, keepdims=True)
        den = w_inters[h] * qn + jnp.sum(s_qks[h], axis=1, keepdims=True)
        hs = num / jnp.maximum(jnp.abs(den), jnp.exp(-m_ts[h]))
        ms = jnp.mean(hs * hs, axis=-1, keepdims=True)
        cols = slice(h * dv, (h + 1) * dv)
        hs = hs * lax.rsqrt(ms + RMS_EPS) * hg_ref[:, cols]
        o_ref[:, cols] = (hs * _sigmoid(og_ref[:, cols].astype(F32))).astype(o_ref.dtype)

        c_ref[h] = c_news[h]
        n_ref[h] = decays[h] * n_ref[h] + jnp.sum(kws[h], axis=0, keepdims=True)
        m_ref[h] = m_news[h]


def _mlstm_cell(proj, gates, bias_row, head_gain, *, batch, seq, heads, dk, dv, chunk):
    t = proj.shape[0]
    nc = seq // chunk
    wq, wv = heads * dk, heads * dv
    assert 2 * wq == wv, "the column-block indices below assume the q|k and v|o-gate widths of the xLSTM ratios"
    row = lambda b, c: b * nc + c
    return pl.pallas_call(
        functools.partial(_mlstm_kernel, heads=heads, dk=dk, dv=dv),
        grid=(batch, nc),
        in_specs=[
            pl.BlockSpec((chunk, wq), lambda b, c: (row(b, c), 0)),
            pl.BlockSpec((chunk, wq), lambda b, c: (row(b, c), 1)),
            pl.BlockSpec((chunk, wv), lambda b, c: (row(b, c), 1)),
            pl.BlockSpec((chunk, wv), lambda b, c: (row(b, c), 2)),
            pl.BlockSpec((chunk, LANES), lambda b, c: (row(b, c), 0)),
            pl.BlockSpec((1, LANES), lambda b, c: (0, 0)),
            pl.BlockSpec((1, wv), lambda b, c: (0, 0)),
        ],
        out_specs=pl.BlockSpec((chunk, wv), lambda b, c: (row(b, c), 0)),
        out_shape=jax.ShapeDtypeStruct((t, wv), BF16),
        scratch_shapes=[pltpu.VMEM((heads, dk, dv), F32), pltpu.VMEM((heads, 1, dk), F32),
                        pltpu.VMEM((heads, 1, 1), F32)],
        compiler_params=_params("parallel", "arbitrary"),
        name="mlstm_cell",
    )(proj, proj, proj, proj, gates, bias_row, head_gain.reshape(1, wv))


def _sb_kernel(q_ref, k_ref, v_ref, o_ref, *, tile, sub):
    i = pl.program_id(2)
    d = q_ref.shape[1]
    row = lax.broadcasted_iota(jnp.int32, (tile, tile), 0)
    col = lax.broadcasted_iota(jnp.int32, (tile, tile), 1)
    later = (row > col).astype(BF16)
    later2 = jnp.concatenate([later, later], axis=0)
    strict = col < row
    qs = [q_ref[r * tile:(r + 1) * tile, :] for r in range(sub)]

    def sweep(kbs, cs, accs, diagonal):
        n_t = range(sub)
        starts = [pl.multiple_of(kb * tile, tile) for kb in kbs]
        zs = [lax.dot_general(qs[r], k_ref[pl.ds(starts[r], tile), :], (((1,), (1,)), ((), ())),
                              preferred_element_type=F32) for r in n_t]
        log_betas, log_keeps, splits = [], [], []
        for r in n_t:
            z = zs[r]
            sp = jnp.log(1.0 + jnp.exp2(jnp.abs(z) * -LOG2_E))
            log_beta = jnp.minimum(z, 0.0) - sp
            log_betas.append(log_beta)
            log_keep = log_beta - z
            if diagonal:
                log_keep = jnp.where(strict, log_keep, 0.0)
            log_keeps.append(log_keep)
            hi = log_keep.astype(BF16)
            lo = (log_keep - hi.astype(F32)).astype(BF16)
            splits.append(jnp.concatenate([hi, lo], axis=1))
        afters = [jnp.dot(splits[r], later2, preferred_element_type=F32) for r in n_t]
        weights = []
        for r in n_t:
            a = jnp.exp(log_betas[r] + afters[r] + cs[r])
            if diagonal:
                a = jnp.where(strict, a, 0.0)
            weights.append(a.astype(BF16))
        new_accs = [accs[r] + jnp.dot(weights[r], v_ref[pl.ds(starts[r], tile), :], preferred_element_type=F32)
                    for r in n_t]
        new_cs = [jnp.where(kbs[r] <= 0, -jnp.inf, cs[r] + afters[r][:, 0:1] + log_keeps[r][:, 0:1]) for r in n_t]
        return new_cs, new_accs

    def unfinished(cs):
        worst = functools.reduce(jnp.maximum, [jnp.max(c, axis=0, keepdims=True) for c in cs])
        return jnp.max(worst) >= F32_EXP_ZERO

    cs, accs = sweep([i * sub + r for r in range(sub)],
                     [jnp.zeros((tile, 1), F32)] * sub, [jnp.zeros((tile, d), F32)] * sub, True)

    def body(carry):
        n, _, cs, accs = carry
        kbs = [jnp.maximum(i * sub + r - n, 0) for r in range(sub)]
        cs, accs = sweep(kbs, cs, accs, False)
        return n + 1, unfinished(cs), tuple(cs), tuple(accs)

    init = (jnp.int32(1), unfinished(cs), tuple(cs), tuple(accs))
    _, _, _, accs = lax.while_loop(lambda carry: carry[1], body, init)
    for r in range(sub):
        o_ref[r * tile:(r + 1) * tile, :] = accs[r].astype(o_ref.dtype)


def _sb_attention(q, kv, *, batch, seq, heads, tile, sub):
    d = SB_HEAD_DIM
    tq = tile * sub
    return pl.pallas_call(
        functools.partial(_sb_kernel, tile=tile, sub=sub),
        grid=(batch, heads, seq // tq),
        in_specs=[
            pl.BlockSpec((None, tq, d), lambda b, h, i: (b, i, h)),
            pl.BlockSpec((None, seq, d), lambda b, h, i: (b, 0, h)),
            pl.BlockSpec((None, seq, d), lambda b, h, i: (b, 0, heads + h)),
        ],
        out_specs=pl.BlockSpec((None, tq, d), lambda b, h, i: (b, i, h)),
        out_shape=jax.ShapeDtypeStruct((batch, seq, heads * d), BF16),
        compiler_params=_params("parallel", "parallel", "arbitrary"),
        name="sb_attention",
    )(q, kv, kv)


def kernel(x, p, ffn_norm, ffn_w_gate, ffn_w_up, ffn_w_down, mix_norm, mlstm_w_in, mlstm_b_if,
           mlstm_head_norm, mlstm_w_out, kv_norm, sb_w_kv, sb_k_norm, sb_w_q, sb_q_norm, sb_w_o,
           ple_norm, ple_w_proj, ple_w_gate):
    batch, seq, d = x.shape
    depth = p.shape[0]
    n_a = mlstm_w_in.shape[0]
    t = batch * seq
    heads = FORGET_HEADS
    dv = mlstm_w_out.shape[1] // heads
    dk = (mlstm_w_in.shape[2] - 2 * heads * dv - 2 * heads) // (2 * heads)
    n_main = 2 * heads * dk + 2 * heads * dv
    sb_heads = sb_w_q.shape[2] // SB_HEAD_DIM
    tm, tmr = ROW_TILE, ROW_TILE_RESIDENT

    wg_all = ffn_w_gate.astype(BF16)
    wu_all = ffn_w_up.astype(BF16)
    wd_all = ffn_w_down.astype(BF16)
    w_in = mlstm_w_in.astype(BF16)
    w_in_gates = jnp.pad(mlstm_w_in[:, :, n_main:].astype(BF16), ((0, 0), (0, 0), (0, LANES - 2 * heads)))
    bias_rows = jnp.pad(mlstm_b_if.astype(F32), ((0, 0), (0, LANES - 2 * heads)))
    w_out = mlstm_w_out.astype(BF16)
    w_kv = sb_w_kv.astype(BF16)
    w_q = sb_w_q.astype(BF16)
    w_o = sb_w_o.astype(BF16)
    w_pp = ple_w_proj.astype(BF16)
    w_pg = ple_w_gate.astype(BF16)

    h = x.reshape(t, d)
    p2 = p.reshape(depth, t, p.shape[-1])
    kv_sh = None
    for i in range(depth):
        if i == n_a:
            kv_sh = _norm_proj(h, kv_norm, w_kv, sb_k_norm, normed_cols=sb_heads * SB_HEAD_DIM,
                               tm=tm, tn=2 * COL_CHUNK, name="kv_proj").reshape(batch, seq, -1)
        h = _ffn(h, ffn_norm[i, 0], wg_all, wu_all, wd_all, layer=i, half=0, tm=tm, tf=FFN_COLS)
        if i < n_a:
            proj, gates = _mlstm_in(h, mix_norm[i], w_in, w_in_gates[i], layer=i, n=n_main, tm=tm, tn=2 * COL_CHUNK)
            hs = _mlstm_cell(proj, gates, bias_rows[i:i + 1], mlstm_head_norm[i], batch=batch, seq=seq,
                             heads=heads, dk=dk, dv=dv, chunk=MLSTM_CHUNK)
            h = _proj_res(hs, w_out, h, layer=i, tm=tmr, name="mlstm_out")
        else:
            j = i - n_a
            q = _norm_proj_resident(h, mix_norm[i], w_q, sb_q_norm[j], layer=j, scale=SB_HEAD_DIM ** -0.5,
                                    tm=tm, name="q_proj")
            o = _sb_attention(q.reshape(batch, seq, -1), kv_sh, batch=batch, seq=seq, heads=sb_heads,
                              tile=SB_TILE, sub=SB_SUB)
            h = _proj_res(o.reshape(t, -1), w_o, h, layer=j, tm=tmr, name="sb_out")
        h = _ffn(h, ffn_norm[i, 1], wg_all, wu_all, wd_all, layer=i, half=1, tm=tm, tf=FFN_COLS)
        h = _ple(h, p2, ple_norm[i], w_pp, w_pg, layer=i, tm=tmr)
    return h.reshape(batch, seq, d)
```
